```python
import math
import jax, jax.numpy as jnp
from jax import lax
import numpy as np

D_MODEL = 1024
BATCH = 4
SEQ = 4096
DEPTH = 4
DEC_BATCH = 32
DEC_SEQ = 4
PAST_LEN = 8192
PAGE_SIZE = 128

D_HEAD = 64
MIX_WIDTH = D_MODEL // 2
A_HEADS = MIX_WIDTH // D_HEAD
A_WIDTH = A_HEADS * D_HEAD
MOBA_BLOCK = 256
MOBA_TOPK = 3
B_HEADS = MIX_WIDTH // D_HEAD
B_WIDTH = B_HEADS * D_HEAD
B_LORA_W = 64
B_LORA_A = 64
C_HEADS = MIX_WIDTH // D_HEAD
C_KV_HEADS = 2
C_WIDTH = C_HEADS * D_HEAD
C_KV_WIDTH = C_KV_HEADS * D_HEAD
IDX_HEADS = 8
IDX_DIM = 64
DSA_TOPK = 256
N_BRANCH = 3
MOBA_ROW = 2 * A_WIDTH
DSA_ROW = 2 * C_KV_WIDTH + IDX_DIM
RW_SIZES = (B_WIDTH, B_WIDTH, B_WIDTH, B_WIDTH, B_LORA_W, B_LORA_A)
RW_COLS = 4 * B_WIDTH + B_LORA_W + B_LORA_A
COL_SIZES = (A_WIDTH, MOBA_ROW, RW_COLS, C_WIDTH, IDX_HEADS * IDX_DIM, IDX_HEADS, DSA_ROW, N_BRANCH * D_MODEL)
D_IN = A_WIDTH + MOBA_ROW + RW_COLS + C_WIDTH + IDX_HEADS * IDX_DIM + IDX_HEADS + DSA_ROW + N_BRANCH * D_MODEL
PEER_KEYS = 128
PEER_EXPERTS = PEER_KEYS * PEER_KEYS
PEER_HEADS = 8
PEER_QDIM = 256
PEER_TOPK = 16
Q_CHUNK_MOBA = 32
Q_CHUNK_DSA = 128
TOK_CHUNK_PEER = 256
EPS = 1e-6
RWKV_GN_EPS = 64e-5
NEG = -1e30

kernel_name = "hybrid_moba_rwkv7_dsa_peer_step"


def rms_norm(x, g):
    xf = x.astype(jnp.float32)
    y = xf * lax.rsqrt(jnp.mean(xf * xf, axis=-1, keepdims=True) + EPS)
    return (y * g.astype(jnp.float32)).astype(x.dtype)


def modulate(x, g, shift, scale):
    return rms_norm(x, g) * (1 + scale[:, None]) + shift[:, None]


def split_sizes(z, sizes):
    return jnp.split(z, np.cumsum(sizes)[:-1].tolist(), axis=-1)


def chunked_map(fn, xs, n, chunk):
    c = math.gcd(n, chunk)
    xs_c = tuple(a.reshape((n // c, c) + a.shape[1:]) for a in xs)
    out = lax.map(lambda args: fn(*args), xs_c)
    return out.reshape((n,) + out.shape[2:])


def moba_attention(q, k, v, q_pos):
    B, T, H, d = q.shape
    L = k.shape[1]
    n_blk = -(-L // MOBA_BLOCK)
    pad = n_blk * MOBA_BLOCK - L
    kp = jnp.pad(k, ((0, 0), (0, pad), (0, 0), (0, 0)))
    vp = jnp.pad(v, ((0, 0), (0, pad), (0, 0), (0, 0)))
    kb = kp.reshape(B, n_blk, MOBA_BLOCK, H, d).transpose(0, 3, 1, 2, 4)
    vb = vp.reshape(B, n_blk, MOBA_BLOCK, H, d).transpose(0, 3, 1, 2, 4)
    k_mean = jnp.mean(kb.astype(jnp.float32), axis=3)
    kk = min(MOBA_TOPK, n_blk)
    scale = d ** -0.5
    bi = jnp.arange(B)[:, None, None, None]
    hi = jnp.arange(H)[None, :, None, None]

    def one_chunk(qc, pc):
        C = qc.shape[0]
        own = pc // MOBA_BLOCK
        gate = jnp.einsum('cbhd,bhnd->bhcn', qc.astype(jnp.float32), k_mean)
        past = jnp.arange(n_blk)[None, :] < own[:, None]
        gate = jnp.where(past[None, None], gate, NEG)
        _, sel = lax.top_k(gate, kk)
        sel_ok = sel < own[None, None, :, None]
        blk = jnp.concatenate([sel, jnp.broadcast_to(own[None, None, :, None], (B, H, C, 1))], axis=-1)
        kg = kb[bi, hi, blk]
        vg = vb[bi, hi, blk]
        s = jnp.einsum('cbhd,bhcjsd->bhcjs', qc, kg).astype(jnp.float32) * scale
        own_pos = blk[..., -1:, None] * MOBA_BLOCK + jnp.arange(MOBA_BLOCK)
        ok = jnp.concatenate([jnp.broadcast_to(sel_ok[..., None], sel_ok.shape + (MOBA_BLOCK,)),
                              own_pos <= pc[None, None, :, None, None]], axis=3)
        s = jnp.where(ok, s, NEG)
        p = jax.nn.softmax(s.reshape(B, H, C, -1), axis=-1).reshape(s.shape)
        return jnp.einsum('bhcjs,bhcjsd->cbhd', p.astype(vg.dtype), vg)

    out = chunked_map(one_chunk, (q.transpose(1, 0, 2, 3), q_pos), T, Q_CHUNK_MOBA)
    return out.transpose(1, 0, 2, 3)


def dsa_attention(q, qi, wi, k, v, ki, q_pos):
    B, T, H, d = q.shape
    L = k.shape[1]
    topk = min(DSA_TOPK, L // 4)
    grp = C_HEADS // C_KV_HEADS
    scale = d ** -0.5
    key_pos = jnp.arange(L)
    bi = jnp.arange(B)[:, None, None]
    ki32 = ki.astype(jnp.float32)

    def one_chunk(qc, qic, wic, pc):
        C = qc.shape[0]
        logits = jnp.einsum('cbhe,ble->bchl', qic.astype(jnp.float32), ki32)
        score = jnp.einsum('bchl,cbh->bcl', jax.nn.relu(logits), wic.astype(jnp.float32) * IDX_HEADS ** -0.5)
        score = jnp.where((key_pos[None, :] <= pc[:, None])[None], score, NEG)
        _, sel = lax.top_k(score, topk)
        ok = sel <= pc[None, :, None]
        kg = k[bi, sel]
        vg = v[bi, sel]
        qg = qc.reshape(C, B, C_KV_HEADS, grp, d)
        s = jnp.einsum('cbgrd,bckgd->bcgrk', qg, kg).astype(jnp.float32) * scale
        s = jnp.where(ok[:, :, None, None, :], s, NEG)
        p = jax.nn.softmax(s, axis=-1)
        o = jnp.einsum('bcgrk,bckgd->cbgrd', p.astype(vg.dtype), vg)
        return o.reshape(C, B, H, d)

    out = chunked_map(one_chunk, (q.transpose(1, 0, 2, 3), qi.transpose(1, 0, 2, 3), wi.transpose(1, 0, 2), q_pos), T, Q_CHUNK_DSA)
    return out.transpose(1, 0, 2, 3)


def rwkv7_time_mix(zr, shift_prev, s0, mu, w0, w_up, a0, a_up, k_k, k_a, r_k, ln_w, ln_b):
    B, T, _ = zr.shape
    shifted = jnp.concatenate([shift_prev[:, None].astype(zr.dtype), zr[:, :-1]], axis=1)
    xm = zr + (shifted - zr) * mu
    r, k, v, g, wl, al = split_sizes(xm, RW_SIZES)
    w_log = -jax.nn.softplus(-(w0 + jnp.tanh(wl) @ w_up)) - 0.5
    decay = jnp.exp(-jnp.exp(w_log.astype(jnp.float32)))
    a = jax.nn.sigmoid((a0 + al @ a_up).astype(jnp.float32))
    heads = lambda t: t.reshape(B, T, B_HEADS, D_HEAD).astype(jnp.float32)
    r, k, v, decay, a = heads(r), heads(k), heads(v), heads(decay), heads(a)
    kk = k * k_k
    kk = kk / jnp.maximum(jnp.sqrt(jnp.sum(kk * kk, axis=-1, keepdims=True)), 1e-12)
    k = k * (1 + (a - 1) * k_a)

    def step(S, xs):
        r_t, k_t, v_t, w_t, kk_t, a_t = xs
        sa = jnp.einsum('bhvk,bhk->bhv', S, -kk_t)
        S = S * w_t[:, :, None, :] + sa[..., None] * (kk_t * a_t)[:, :, None, :] + v_t[..., None] * k_t[:, :, None, :]
        return S, jnp.einsum('bhvk,bhk->bhv', S, r_t)

    xs = tuple(t.transpose(1, 0, 2, 3) for t in (r, k, v, decay, kk, a))
    s_final, y = lax.scan(step, s0.astype(jnp.float32), xs)
    y = y.transpose(1, 0, 2, 3)
    mean = jnp.mean(y, axis=-1, keepdims=True)
    var = jnp.mean(jnp.square(y - mean), axis=-1, keepdims=True)
    y = ((y - mean) * lax.rsqrt(var + RWKV_GN_EPS)).reshape(B, T, B_WIDTH) * ln_w + ln_b
    bonus = jnp.sum(r * k * r_k, axis=-1, keepdims=True) * v
    y = (y + bonus.reshape(B, T, B_WIDTH)) * jax.nn.sigmoid(g.astype(jnp.float32))
    return y.astype(zr.dtype), s_final, zr[:, -1]


def peer_ffn(x, w_q, sub_keys, u, v):
    N = x.shape[0]

    def one_chunk(xc):
        n = xc.shape[0]
        q = (xc @ w_q).reshape(n, PEER_HEADS, 2, PEER_QDIM // 2).astype(jnp.float32)
        s = jnp.einsum('nhpe,hpke->nhpk', q, sub_keys.astype(jnp.float32))
        s1, i1 = lax.top_k(s[:, :, 0], PEER_TOPK)
        s2, i2 = lax.top_k(s[:, :, 1], PEER_TOPK)
        cand = (s1[..., :, None] + s2[..., None, :]).reshape(n, PEER_HEADS, -1)
        cid = (i1[..., :, None] * PEER_KEYS + i2[..., None, :]).reshape(n, PEER_HEADS, -1)
        best, pos = lax.top_k(cand, PEER_TOPK)
        eid = jnp.take_along_axis(cid, pos, axis=-1)
        gate = jax.nn.softmax(best, axis=-1)
        ue = u[eid]
        ve = v[eid]
        act = jax.nn.gelu(jnp.einsum('nd,nhed->nhe', xc, ue).astype(jnp.float32), approximate=False)
        return jnp.einsum('nhe,nhed->nd', (gate * act).astype(ve.dtype), ve)

    return chunked_map(one_chunk, (x,), N, TOK_CHUNK_PEER)


def block(x, c, q_pos, past_moba, past_dsa, shift_prev, rw_state,
          w_ada, b_ada, norm_mix, w_in, rwkv_mu, rwkv_w0, rwkv_w_up, rwkv_a0, rwkv_a_up,
          rwkv_k_k, rwkv_k_a, rwkv_r_k, rwkv_ln_w, rwkv_ln_b, w_branch, w_out,
          norm_ffn, peer_wq, peer_keys, peer_u, peer_v):
    B, T, D = x.shape
    sh1, sc1, gt1, sh2, sc2, gt2 = jnp.split(c @ w_ada + b_ada, 6, axis=-1)
    h = modulate(x, norm_mix, sh1, sc1)
    z = h @ w_in
    qa, moba_row, zr, qc, qi, wi, dsa_row, gl = split_sizes(z, COL_SIZES)
    rows_a = moba_row if past_moba is None else jnp.concatenate([past_moba.astype(moba_row.dtype), moba_row], axis=1)
    La = rows_a.shape[1]
    ka, va = jnp.split(rows_a, 2, axis=-1)
    ya = moba_attention(qa.reshape(B, T, A_HEADS, D_HEAD), ka.reshape(B, La, A_HEADS, D_HEAD),
                        va.reshape(B, La, A_HEADS, D_HEAD), q_pos)
    yb, rw_new, shift_new = rwkv7_time_mix(zr, shift_prev, rw_state, rwkv_mu, rwkv_w0, rwkv_w_up, rwkv_a0,
                                           rwkv_a_up, rwkv_k_k, rwkv_k_a, rwkv_r_k, rwkv_ln_w, rwkv_ln_b)
    rows_c = dsa_row if past_dsa is None else jnp.concatenate([past_dsa.astype(dsa_row.dtype), dsa_row], axis=1)
    Lc = rows_c.shape[1]
    kc, vc, kic = split_sizes(rows_c, (C_KV_WIDTH, C_KV_WIDTH, IDX_DIM))
    yc = dsa_attention(qc.reshape(B, T, C_HEADS, D_HEAD), qi.reshape(B, T, IDX_HEADS, IDX_DIM), wi,
                       kc.reshape(B, Lc, C_KV_HEADS, D_HEAD), vc.reshape(B, Lc, C_KV_HEADS, D_HEAD), kic, q_pos)
    ys = jnp.stack([ya.reshape(B, T, MIX_WIDTH), yb, yc.reshape(B, T, MIX_WIDTH)], axis=0)
    proj = jnp.einsum('nbtw,nwd->nbtd', ys, w_branch)
    gates = jax.nn.sigmoid(gl.reshape(B, T, N_BRANCH, D)).transpose(2, 0, 1, 3)
    mix = jnp.sum(gates * proj, axis=0) @ w_out
    x = x + gt1[:, None] * mix
    h2 = modulate(x, norm_ffn, sh2, sc2)
    x = x + gt2[:, None] * peer_ffn(h2.reshape(B * T, D), peer_wq, peer_keys, peer_u, peer_v).reshape(B, T, D)
    return x, moba_row, dsa_row, rw_new, shift_new


def setup_inputs(seed: int = 0) -> dict:
    key = jax.random.key(seed)
    ks = iter(jax.random.split(key, 48))
    D = D_MODEL
    nrm = lambda shape, s: jax.random.normal(next(ks), shape, jnp.float32) * s
    uni = lambda shape: jax.random.uniform(next(ks), shape, jnp.float32)
    n_pages = PAST_LEN // PAGE_SIZE
    n_used = DEC_BATCH * n_pages
    n_pool = n_used + n_used // 4
    inp = {}
    inp["x_prompt"] = nrm((BATCH, SEQ, D), 1.0)
    inp["x_sample"] = nrm((DEC_BATCH, DEC_SEQ, D), 1.0)
    inp["cache_moba"] = nrm((DEPTH, n_pool, PAGE_SIZE, MOBA_ROW), 1.0)
    inp["cache_dsa"] = nrm((DEPTH, n_pool, PAGE_SIZE, DSA_ROW), 1.0)
    inp["state_rwkv"] = nrm((DEPTH, DEC_BATCH, B_HEADS, D_HEAD, D_HEAD), 0.3)
    inp["state_shift"] = nrm((DEPTH, DEC_BATCH, RW_COLS), 1.0)
    inp["page_table"] = jax.random.permutation(next(ks), n_pool)[:n_used].reshape(DEC_BATCH, n_pages).astype(jnp.int32)
    inp["c_prompt"] = nrm((BATCH, D), 1.0)
    inp["c_sample"] = nrm((DEC_BATCH, D), 1.0)
    inp["w_ada"] = nrm((DEPTH, D, 6 * D), 0.3 * D ** -0.5)
    inp["b_ada"] = nrm((DEPTH, 6 * D), 0.01)
    inp["norm_mix"] = 1.0 + nrm((DEPTH, D), 0.05)
    inp["w_in"] = nrm((DEPTH, D, D_IN), D ** -0.5)
    inp["rwkv_mu"] = uni((DEPTH, RW_COLS))
    inp["rwkv_w0"] = -6.0 + 5.0 * uni((DEPTH, B_WIDTH))
    inp["rwkv_w_up"] = nrm((DEPTH, B_LORA_W, B_WIDTH), 0.5 * B_LORA_W ** -0.5)
    inp["rwkv_a0"] = nrm((DEPTH, B_WIDTH), 0.5)
    inp["rwkv_a_up"] = nrm((DEPTH, B_LORA_A, B_WIDTH), B_LORA_A ** -0.5)
    inp["rwkv_k_k"] = 0.85 + nrm((DEPTH, B_HEADS, D_HEAD), 0.05)
    inp["rwkv_k_a"] = 1.0 + nrm((DEPTH, B_HEADS, D_HEAD), 0.05)
    inp["rwkv_r_k"] = nrm((DEPTH, B_HEADS, D_HEAD), 0.1)
    inp["rwkv_ln_w"] = 1.0 + nrm((DEPTH, B_WIDTH), 0.05)
    inp["rwkv_ln_b"] = nrm((DEPTH, B_WIDTH), 0.01)
    inp["w_branch"] = nrm((DEPTH, N_BRANCH, MIX_WIDTH, D), MIX_WIDTH ** -0.5)
    inp["w_out"] = nrm((DEPTH, D, D), D ** -0.5)
    inp["norm_ffn"] = 1.0 + nrm((DEPTH, D), 0.05)
    inp["peer_wq"] = nrm((DEPTH, D, PEER_HEADS * PEER_QDIM), D ** -0.5)
    inp["peer_keys"] = nrm((DEPTH, PEER_HEADS, 2, PEER_KEYS, PEER_QDIM // 2), (PEER_QDIM // 2) ** -0.5)
    inp["peer_u"] = nrm((DEPTH, PEER_EXPERTS, D), D ** -0.5)
    inp["peer_v"] = nrm((DEPTH, PEER_EXPERTS, D), PEER_HEADS ** -0.5)
    inp["norm_final"] = 1.0 + nrm((D,), 0.05)
    return inp


def reference(x_prompt, x_sample, cache_moba, cache_dsa, state_rwkv, state_shift, page_table, c_prompt, c_sample,
              w_ada, b_ada, norm_mix, w_in, rwkv_mu, rwkv_w0, rwkv_w_up, rwkv_a0, rwkv_a_up, rwkv_k_k, rwkv_k_a,
              rwkv_r_k, rwkv_ln_w, rwkv_ln_b, w_branch, w_out, norm_ffn, peer_wq, peer_keys, peer_u, peer_v,
              norm_final):
    n_b, n_t = x_prompt.shape[0], x_prompt.shape[1]
    n_db, n_dt = x_sample.shape[0], x_sample.shape[1]
    past_len = page_table.shape[1] * PAGE_SIZE
    pos_p = jnp.arange(n_t, dtype=jnp.int32)
    pos_s = past_len + jnp.arange(n_dt, dtype=jnp.int32)
    xp, xs = x_prompt, x_sample
    mp, ms, dp, ds, rp, rs, sp, ss = [], [], [], [], [], [], [], []
    for l in range(DEPTH):
        lp = (w_ada[l], b_ada[l], norm_mix[l], w_in[l], rwkv_mu[l], rwkv_w0[l], rwkv_w_up[l], rwkv_a0[l],
              rwkv_a_up[l], rwkv_k_k[l], rwkv_k_a[l], rwkv_r_k[l], rwkv_ln_w[l], rwkv_ln_b[l], w_branch[l], w_out[l],
              norm_ffn[l], peer_wq[l], peer_keys[l], peer_u[l], peer_v[l])
        zero_shift = jnp.zeros((n_b, RW_COLS), x_prompt.dtype)
        zero_state = jnp.zeros((n_b, B_HEADS, D_HEAD, D_HEAD), jnp.float32)
        xp, m_p, d_p, r_p, s_p = block(xp, c_prompt, pos_p, None, None, zero_shift, zero_state, *lp)
        past_m = cache_moba[l, page_table].reshape(n_db, past_len, MOBA_ROW)
        past_d = cache_dsa[l, page_table].reshape(n_db, past_len, DSA_ROW)
        xs, m_s, d_s, r_s, s_s = block(xs, c_sample, pos_s, past_m, past_d, state_shift[l], state_rwkv[l], *lp)
        mp.append(m_p); ms.append(m_s); dp.append(d_p); ds.append(d_s)
        rp.append(r_p); rs.append(r_s); sp.append(s_p); ss.append(s_s)
    y_prompt = rms_norm(xp, norm_final)
    y_sample = rms_norm(xs, norm_final)
    return (y_prompt, y_sample, jnp.stack(mp), jnp.stack(ms), jnp.stack(dp), jnp.stack(ds),
            jnp.stack(rp), jnp.stack(rs), jnp.stack(sp), jnp.stack(ss))
```

```python
import functools

import jax
import jax.numpy as jnp
import numpy as np
from jax import lax
from jax.experimental import pallas as pl
from jax.experimental.pallas import tpu as pltpu

D_HEAD = 64
MOBA_BLOCK = 256
MOBA_TOPK = 3
B_LORA = 64
C_KV_HEADS = 2
IDX_HEADS = 8
IDX_DIM = 64
DSA_TOPK = 256
N_BRANCH = 3
PEER_KEYS = 128
PEER_HEADS = 8
PEER_TOPK = 16
PAGE_SIZE = 128
EPS = 1e-6
RWKV_GN_EPS = 64e-5
NEG = -1e30

LANES = 128
SUBLANES = 8
VMEM_LIMIT_BYTES = 56 * 1024 * 1024

F32 = jnp.float32
BF16 = jnp.bfloat16
INT_MIN = -2 ** 31


def _params(*sem):
    return pltpu.CompilerParams(dimension_semantics=sem, vmem_limit_bytes=VMEM_LIMIT_BYTES)


def _split_bf16(a):
    hi = a.astype(BF16)
    lo = (a - hi.astype(F32)).astype(BF16)
    return hi, lo


def _dot3(a, b, dims):
    ah, al = _split_bf16(a)
    bh, bl = _split_bf16(b)
    dn = (dims, ((), ()))
    d = lambda x, y: lax.dot_general(x, y, dn, preferred_element_type=F32)
    return d(ah, bh) + (d(ah, bl) + d(al, bh))


_NN = ((1,), (0,))
_NT = ((1,), (1,))


def _dot_bf16(a, b, dims=_NN):
    return lax.dot_general(a.astype(BF16), b.astype(BF16), (dims, ((), ())), preferred_element_type=F32)


def _row_tile(n, want):
    t = min(n, want)
    while n % t:
        t //= 2
    return t


def _modulate_kernel(x_ref, sc_ref, sh_ref, g_ref, o_ref):
    x = x_ref[0]
    y = x * lax.rsqrt(jnp.mean(x * x, axis=-1, keepdims=True) + EPS) * g_ref[...]
    o_ref[0] = (y * (1.0 + sc_ref[0]) + sh_ref[0]).astype(o_ref.dtype)


def modulate(x, scale, shift, g, out_dtype):
    B, T, D = x.shape
    tm = _row_tile(T, 512)
    tmod = tm if scale.shape[1] == T else 1
    mod_map = (lambda b, i: (b, i, 0)) if scale.shape[1] == T else (lambda b, i: (b, 0, 0))
    return pl.pallas_call(
        _modulate_kernel,
        grid=(B, T // tm),
        in_specs=[pl.BlockSpec((1, tm, D), lambda b, i: (b, i, 0)),
                  pl.BlockSpec((1, tmod, D), mod_map),
                  pl.BlockSpec((1, tmod, D), mod_map),
                  pl.BlockSpec((1, D), lambda b, i: (0, 0))],
        out_specs=pl.BlockSpec((1, tm, D), lambda b, i: (b, i, 0)),
        out_shape=jax.ShapeDtypeStruct((B, T, D), out_dtype),
        compiler_params=_params("parallel", "parallel"),
        name="modulate",
    )(x, scale, shift, g.reshape(1, D))


def _rmsnorm_kernel(x_ref, g_ref, o_ref):
    x = x_ref[...]
    o_ref[...] = x * lax.rsqrt(jnp.mean(x * x, axis=-1, keepdims=True) + EPS) * g_ref[...]


def rms_norm_rows(x, g):
    N, D = x.shape
    tm = _row_tile(N, 512)
    return pl.pallas_call(
        _rmsnorm_kernel,
        grid=(N // tm,),
        in_specs=[pl.BlockSpec((tm, D), lambda i: (i, 0)), pl.BlockSpec((1, D), lambda i: (0, 0))],
        out_specs=pl.BlockSpec((tm, D), lambda i: (i, 0)),
        out_shape=jax.ShapeDtypeStruct((N, D), F32),
        compiler_params=_params("parallel"),
        name="final_norm",
    )(x, g.reshape(1, D))


def _mm_kernel(a_ref, w_ref, o_ref):
    o_ref[...] = jnp.dot(a_ref[...], w_ref[...], preferred_element_type=F32).astype(o_ref.dtype)


def _mm3_kernel(a_ref, w_ref, b_ref, o_ref):
    o_ref[0] = _dot3(a_ref[...], w_ref[0], _NN) + b_ref[0]


def matmul(a, w, out_dtype=F32, tn_want=1024):
    N, K = a.shape
    C = w.shape[1]
    tm = _row_tile(N, 512)
    tn = _row_tile(C, tn_want)
    return pl.pallas_call(
        _mm_kernel,
        grid=(C // tn, N // tm),
        in_specs=[pl.BlockSpec((tm, K), lambda j, i: (i, 0)), pl.BlockSpec((K, tn), lambda j, i: (0, j))],
        out_specs=pl.BlockSpec((tm, tn), lambda j, i: (i, j)),
        out_shape=jax.ShapeDtypeStruct((N, C), out_dtype),
        compiler_params=_params("parallel", "parallel"),
        name="matmul",
    )(a, w)


def ada_matmul(c, w, b):
    M, D = c.shape
    L, _, C = w.shape
    tn = _row_tile(C, 1536)
    return pl.pallas_call(
        _mm3_kernel,
        grid=(L, C // tn),
        in_specs=[pl.BlockSpec((M, D), lambda l, j: (0, 0)),
                  pl.BlockSpec((1, D, tn), lambda l, j: (l, 0, j)),
                  pl.BlockSpec((1, 1, tn), lambda l, j: (l, 0, j))],
        out_specs=pl.BlockSpec((1, M, tn), lambda l, j: (l, 0, j)),
        out_shape=jax.ShapeDtypeStruct((L, M, C), F32),
        compiler_params=_params("parallel", "parallel"),
        name="ada_matmul",
    )(c, w, b.reshape(L, 1, C))


def _paged_cast_kernel(tbl_ref, p0_ref, p1_ref, tail_ref, o_ref, *mean_ref, n_steps, mean_cols):
    j = pl.program_id(1)

    def emit(rows):
        o_ref[0] = rows.astype(o_ref.dtype)
        if mean_ref:
            mean_ref[0][0] = jnp.mean(rows[:, :mean_cols], axis=0, keepdims=True)

    @pl.when(j < n_steps)
    def _():
        emit(jnp.concatenate([p0_ref[0, 0], p1_ref[0, 0]], axis=0))

    @pl.when(j >= n_steps)
    def _():
        emit(tail_ref[0])


def paged_cast(pages, layer, table, tail, mean_cols=0):
    _, _, page, R = pages.shape
    B, n_pg = table.shape
    assert page == PAGE_SIZE and MOBA_BLOCK == 2 * PAGE_SIZE and n_pg % 2 == 0
    n_steps = n_pg // 2
    has_tail = tail is not None
    n_blk = n_steps + (1 if has_tail else 0)
    if not has_tail:
        tail = jnp.zeros((B, MOBA_BLOCK, R), F32)
    last = n_steps - 1

    def page_map(k):
        return lambda b, j, tbl: (layer, tbl[b, 2 * jnp.minimum(j, last) + k], 0, 0)

    out_shape = [jax.ShapeDtypeStruct((B, n_blk * MOBA_BLOCK, R), BF16)]
    out_specs = [pl.BlockSpec((1, MOBA_BLOCK, R), lambda b, j, tbl: (b, j, 0))]
    if mean_cols:
        out_shape.append(jax.ShapeDtypeStruct((B * n_blk, 1, mean_cols), F32))
        out_specs.append(pl.BlockSpec((1, 1, mean_cols), lambda b, j, tbl: (b * n_blk + j, 0, 0)))
    res = pl.pallas_call(
        functools.partial(_paged_cast_kernel, n_steps=n_steps, mean_cols=mean_cols),
        grid_spec=pltpu.PrefetchScalarGridSpec(
            num_scalar_prefetch=1,
            grid=(B, n_blk),
            in_specs=[pl.BlockSpec((1, 1, PAGE_SIZE, R), page_map(0)),
                      pl.BlockSpec((1, 1, PAGE_SIZE, R), page_map(1)),
                      pl.BlockSpec((1, MOBA_BLOCK, R), lambda b, j, tbl: (b, 0, 0))],
            out_specs=out_specs),
        out_shape=out_shape,
        compiler_params=_params("parallel", "arbitrary"),
        name="paged_cast",
    )(table, pages, pages, tail)
    if mean_cols:
        return res[0], res[1].reshape(B, n_blk, mean_cols)
    return res[0]


def _moba_kernel(q_ref, k_ref, v_ref, km_ref, o_ref, *, pos0, tq):
    i = pl.program_id(2)
    L = k_ref.shape[1]
    nbp = km_ref.shape[1]
    q = q_ref[0] * (D_HEAD ** -0.5)
    k = k_ref[0]
    v = v_ref[0]
    km = km_ref[0]
    lane = lax.broadcasted_iota(jnp.int32, (1, LANES), 1)
    q_pos = pos0 + i * tq + lax.broadcasted_iota(jnp.int32, (tq, 1), 0)
    own = q_pos // MOBA_BLOCK
    key_pos = lax.broadcasted_iota(jnp.int32, (1, L), 1)
    causal = jnp.where(key_pos > q_pos, NEG, 0.0)
    blk_id = lax.broadcasted_iota(jnp.int32, (tq, nbp), 1).astype(F32)
    own = own.astype(F32)
    expand =jnp.where(lax.broadcasted_iota(jnp.int32, (nbp, L), 1) // MOBA_BLOCK
                       == lax.broadcasted_iota(jnp.int32, (nbp, L), 0), 1.0, 0.0).astype(BF16)
    past = blk_id < own
    outs = []
    for h in range(2):
        qh = jnp.where(lane // D_HEAD == h, q, 0.0)
        g = jnp.where(past, _dot3(qh, km, _NT), NEG)
        closed = jnp.where(blk_id == own, 0.0, NEG)
        for _ in range(MOBA_TOPK):
            m = jnp.max(g, axis=-1, keepdims=True)
            first = jnp.min(jnp.where(g == m, blk_id, nbp), axis=-1, keepdims=True)
            pick = blk_id == first
            closed = jnp.where(pick & past, 0.0, closed)
            g = jnp.where(pick, -jnp.inf, g)
        s = _dot_bf16(qh, k, _NT) + _dot_bf16(closed, expand) + causal
        p = jnp.exp(s - jnp.max(s, axis=-1, keepdims=True))
        o = _dot_bf16(p, v) / jnp.sum(p, axis=-1, keepdims=True)
        outs.append(o)
    o_ref[0] = jnp.where(lane // D_HEAD == 0, outs[0], outs[1])


def moba_attention(q3, kv, km, pos0):
    B, T, _ = q3.shape
    L = kv.shape[1]
    nbp = km.shape[1]
    tq = _row_tile(T, 128)
    n_pair = kv.shape[2] // (2 * LANES)
    return pl.pallas_call(
        functools.partial(_moba_kernel, pos0=pos0, tq=tq),
        grid=(B, n_pair, T // tq),
        in_specs=[pl.BlockSpec((1, tq, LANES), lambda b, hp, i: (b, i, hp)),
                  pl.BlockSpec((1, L, LANES), lambda b, hp, i: (b, 0, hp)),
                  pl.BlockSpec((1, L, LANES), lambda b, hp, i: (b, 0, n_pair + hp)),
                  pl.BlockSpec((1, nbp, LANES), lambda b, hp, i: (b, 0, hp))],
        out_specs=pl.BlockSpec((1, tq, LANES), lambda b, hp, i: (b, i, hp)),
        out_shape=jax.ShapeDtypeStruct((B, T, n_pair * LANES), F32),
        compiler_params=_params("parallel", "parallel", "parallel"),
        name="moba_attention",
    )(q3, kv, kv, km)


DSA_ROW = 2 * C_KV_HEADS * D_HEAD + IDX_DIM


def _dsa_kernel(qc_ref, qi_ref, wi_ref, kv_ref, o_ref, key_scr, bias_scr, *, pos0, tq, topk, n_heads):
    i = pl.program_id(1)
    L = kv_ref.shape[1]
    kvw = C_KV_HEADS * D_HEAD
    k2 = kv_ref[0, :, 0:kvw]
    v2 = kv_ref[0, :, kvw:2 * kvw]
    ki = kv_ref[0, :, 2 * kvw:2 * kvw + IDX_DIM]
    qc = qc_ref[0] * (D_HEAD ** -0.5)
    qi = qi_ref[0]
    wi = wi_ref[0][:, DSA_ROW:DSA_ROW + IDX_HEADS] * (IDX_HEADS ** -0.5)
    q_pos = pos0 + i * tq + lax.broadcasted_iota(jnp.int32, (tq, 1), 0)
    key_pos = lax.broadcasted_iota(jnp.int32, (1, L), 1)
    valid = key_pos <= q_pos

    score = jnp.zeros((tq, L), F32)
    for h in range(IDX_HEADS):
        lg = _dot_bf16(qi[:, h * IDX_DIM:(h + 1) * IDX_DIM], ki, _NT)
        score = score + jnp.maximum(lg, 0.0) * wi[:, h:h + 1]
    score = jnp.where(valid, score, NEG)
    score = jnp.where(score == 0.0, 0.0, score)
    bits = pltpu.bitcast(score, jnp.int32)
    key = jnp.where(bits < 0, bits ^ 0x7FFFFFFF, bits)
    key_scr[...] = key

    def count_ge(c):
        return jnp.sum(jnp.where(key_scr[...] >= c, 1.0, 0.0), axis=-1, keepdims=True)

    t0 = jnp.where(count_ge(jnp.zeros((tq, 1), jnp.int32)) >= topk, 0, INT_MIN).astype(jnp.int32)

    def bit_step(it, t):
        c = t | lax.shift_left(jnp.int32(1), 30 - it)
        return jnp.where(count_ge(c) >= topk, c, t)

    thr = lax.fori_loop(0, 31, bit_step, t0)
    key = key_scr[...]
    take = (key >= thr) & valid
    bias_scr[...] = jnp.where(take, 0.0, NEG)
    n_take = jnp.sum(jnp.where(take, 1.0, 0.0), axis=-1, keepdims=True)

    @pl.when(jnp.max(n_take) > topk)
    def _():
        above = (key > thr) & valid
        need = topk - jnp.sum(jnp.where(above, 1.0, 0.0), axis=-1, keepdims=True)
        upper = jnp.where(lax.broadcasted_iota(jnp.int32, (LANES, LANES), 0)
                          < lax.broadcasted_iota(jnp.int32, (LANES, LANES), 1), 1.0, 0.0).astype(BF16)

        def blk(j, seen):
            sl = pl.ds(pl.multiple_of(j * LANES, LANES), LANES)
            kj = key_scr[:, sl]
            ok = (j * LANES + lax.broadcasted_iota(jnp.int32, (1, LANES), 1)) <= q_pos
            tied = jnp.where((kj == thr) & ok, 1.0, 0.0)
            before = seen + _dot_bf16(tied, upper)
            keep = ((kj > thr) & ok) | ((tied > 0.0) & (before < need))
            bias_scr[:, sl] = jnp.where(keep, 0.0, NEG)
            return seen + jnp.sum(tied, axis=-1, keepdims=True)

        lax.fori_loop(0, L // LANES, blk, jnp.zeros((tq, 1), F32))

    bias = bias_scr[...]
    grp = n_heads // C_KV_HEADS
    zero = jnp.zeros((tq, D_HEAD), F32)
    outs = []
    for h in range(n_heads):
        g = h // grp
        qh = qc[:, h * D_HEAD:(h + 1) * D_HEAD]
        q128 = jnp.concatenate([qh, zero] if g == 0 else [zero, qh], axis=1)
        s = _dot_bf16(q128, k2, _NT) + bias
        p = jnp.exp(s - jnp.max(s, axis=-1, keepdims=True))
        o = _dot_bf16(p, v2) / jnp.sum(p, axis=-1, keepdims=True)
        outs.append(o[:, g * D_HEAD:(g + 1) * D_HEAD])
    o_ref[0] = jnp.concatenate(outs, axis=1)


def dsa_attention(q3, drow, kv, pos0, n_keys):
    B, T, _ = q3.shape
    L, R = kv.shape[1], kv.shape[2]
    W = q3.shape[2] // 3
    tq = _row_tile(T, 128)
    topk = min(DSA_TOPK, n_keys // 4)
    return pl.pallas_call(
        functools.partial(_dsa_kernel, pos0=pos0, tq=tq, topk=topk, n_heads=W // D_HEAD),
        grid=(B, T // tq),
        in_specs=[pl.BlockSpec((1, tq, W), lambda b, i: (b, i, 1)),
                  pl.BlockSpec((1, tq, W), lambda b, i: (b, i, 2)),
                  pl.BlockSpec((1, tq, drow.shape[2]), lambda b, i: (b, i, 0)),
                  pl.BlockSpec((1, L, R), lambda b, i: (b, 0, 0))],
        out_specs=pl.BlockSpec((1, tq, W), lambda b, i: (b, i, 0)),
        out_shape=jax.ShapeDtypeStruct((B, T, W), F32),
        scratch_shapes=[pltpu.VMEM((tq, L), jnp.int32), pltpu.VMEM((tq, L), F32)],
        compiler_params=_params("parallel", "parallel"),
        name="dsa_attention",
    )(q3, q3, drow, kv)


def _segsum(x, ones_blk):
    hi = x.astype(BF16)
    r1 = x - hi.astype(F32)
    mid = r1.astype(BF16)
    lo = (r1 - mid.astype(F32)).astype(BF16)
    d = lambda a: jnp.dot(a, ones_blk, preferred_element_type=F32)
    return d(hi) + (d(mid) + d(lo))


def _rwkv_prep_kernel(z_ref, zs_ref, mu_ref, lora_ref, vec_ref, ones_ref, rs_ref, gb_ref, *, width):
    W = width
    z = z_ref[...]
    xm = z + (zs_ref[...] - z) * mu_ref[...]
    r, k, v, g = (xm[:, n * W:(n + 1) * W] for n in range(4))
    lat = xm[:, 4 * W:4 * W + 2 * B_LORA]
    lane = lax.broadcasted_iota(jnp.int32, (1, 2 * B_LORA), 1)
    lat = jnp.where(lane < B_LORA, jnp.tanh(lat), lat)
    up = _dot3(lat, lora_ref[...], _NN)
    w0, a0, k_k, k_a, r_k = (vec_ref[n:n + 1, :] for n in range(5))
    wpre = -(w0 + up[:, :W])
    softplus = jnp.maximum(wpre, 0.0) + jnp.log(1.0 + jnp.exp(-jnp.abs(wpre)))
    decay = jnp.exp(-jnp.exp(-softplus - 0.5))
    a = jax.nn.sigmoid(a0 + up[:, W:])
    ones_blk = ones_ref[...]
    kk = k * k_k
    kk = kk / jnp.maximum(jnp.sqrt(_segsum(kk * kk, ones_blk)), 1e-12)
    k = k * (1.0 + (a - 1.0) * k_a)
    bonus = _segsum(r * k * r_k, ones_blk) * v
    rs_ref[...] = jnp.concatenate([r, decay, k, kk, kk * a, v], axis=1)
    gb_ref[...] = jnp.concatenate([jax.nn.sigmoid(g), bonus], axis=1)


def rwkv_prep(zr, zs, mu, lora, vecs, ones_blk):
    N, C = zr.shape
    W = vecs.shape[1]
    tm = _row_tile(N, 256)
    full = lambda a: pl.BlockSpec(a.shape, lambda i: (0,) * a.ndim)
    return pl.pallas_call(
        functools.partial(_rwkv_prep_kernel, width=W),
        grid=(N // tm,),
        in_specs=[pl.BlockSpec((tm, C), lambda i: (i, 0)), pl.BlockSpec((tm, C), lambda i: (i, 0)),
                  full(mu), full(lora), full(vecs), full(ones_blk)],
        out_specs=[pl.BlockSpec((tm, 6 * W), lambda i: (i, 0)), pl.BlockSpec((tm, 2 * W), lambda i: (i, 0))],
        out_shape=[jax.ShapeDtypeStruct((N, 6 * W), F32), jax.ShapeDtypeStruct((N, 2 * W), F32)],
        compiler_params=_params("parallel"),
        name="rwkv_prep",
    )(zr, zs, mu, lora, vecs, ones_blk)


def _rwkv_scan_kernel(rs_ref, gb_ref, s0_ref, lnw_ref, lnb_ref, y_ref, sn_ref, s_scr, yt_scr,
                      *, bb, tc, t_total, width):
    c = pl.program_id(1)
    n_hp = width // LANES
    n_pair = bb * n_hp

    @pl.when(c == 0)
    def _():
        s_scr[...] = s0_ref[...]

    yt_scr[...] = jnp.zeros(yt_scr.shape, F32)
    lane = lax.broadcasted_iota(jnp.int32, (1, LANES), 1)
    left = lane < D_HEAD
    eye2 = jnp.where(lax.broadcasted_iota(jnp.int32, (D_HEAD, LANES), 0)
                     == lax.broadcasted_iota(jnp.int32, (D_HEAD, LANES), 1) % D_HEAD, 1.0, 0.0)
    tok = lax.broadcasted_iota(jnp.int32, (1, tc), 1)

    def halves(p):
        return (jnp.sum(jnp.where(left, p, 0.0), axis=-1, keepdims=True),
                jnp.sum(jnp.where(left, 0.0, p), axis=-1, keepdims=True))

    def step(g, carry):
        t0 = pl.multiple_of(g * SUBLANES, SUBLANES)
        for p in range(n_pair):
            b, hp = divmod(p, n_hp)
            grp = [rs_ref[b, pl.ds(t0, SUBLANES), pl.ds(n * width + hp * LANES, LANES)] for n in range(6)]
            s = s_scr[p]
            y_l, y_r = yt_scr[2 * p], yt_scr[2 * p + 1]
            for j in range(SUBLANES):
                r, w, k, kk, kka, v = (a[j:j + 1] for a in grp)
                sa_l, sa_r = halves(s * kk)
                v_l, v_r = halves(eye2 * v)
                s = s * w - jnp.where(left, sa_l, sa_r) * kka + jnp.where(left, v_l, v_r) * k
                o_l, o_r = halves(s * r)
                here = tok == t0 + j
                y_l = jnp.where(here, o_l, y_l)
                y_r = jnp.where(here, o_r, y_r)
            s_scr[p] = s
            yt_scr[2 * p], yt_scr[2 * p + 1] = y_l, y_r
        return carry

    n_tok = jnp.minimum(tc, t_total - c * tc)
    lax.fori_loop(0, (n_tok + SUBLANES - 1) // SUBLANES, step, 0)

    for p in range(n_pair):
        b, hp = divmod(p, n_hp)
        rows = []
        for h2 in range(2):
            y = yt_scr[2 * p + h2]
            mean = jnp.mean(y, axis=0, keepdims=True)
            var = jnp.mean(jnp.square(y - mean), axis=0, keepdims=True)
            seg = pl.ds(hp * LANES + h2 * D_HEAD, D_HEAD)
            rows.append((y - mean) * lax.rsqrt(var + RWKV_GN_EPS) * lnw_ref[seg, :] + lnb_ref[seg, :])
        yn = jnp.concatenate(rows, axis=0).T
        cols = pl.ds(hp * LANES, LANES)
        y_ref[b, :, cols] = (yn + gb_ref[b, :, pl.ds(width + hp * LANES, LANES)]) * gb_ref[b, :, cols]

    @pl.when(c == pl.num_programs(1) - 1)
    def _():
        sn_ref[...] = s_scr[...]


def rwkv_scan(rs, gb, s0, ln_w, ln_b, t_total):
    B, Tp, _ = rs.shape
    W = gb.shape[2] // 2
    n_hp = W // LANES
    tc = LANES
    bb = min(B, 4)
    assert B % bb == 0 and Tp % tc == 0
    lnw = jnp.broadcast_to(ln_w.reshape(W, 1), (W, tc))
    lnb = jnp.broadcast_to(ln_b.reshape(W, 1), (W, tc))
    return pl.pallas_call(
        functools.partial(_rwkv_scan_kernel, bb=bb, tc=tc, t_total=t_total, width=W),
        grid=(B // bb, Tp // tc),
        in_specs=[pl.BlockSpec((bb, tc, 6 * W), lambda g, c: (g, c, 0)),
                  pl.BlockSpec((bb, tc, 2 * W), lambda g, c: (g, c, 0)),
                  pl.BlockSpec((bb * n_hp, D_HEAD, LANES), lambda g, c: (g, 0, 0)),
                  pl.BlockSpec((W, tc), lambda g, c: (0, 0)),
                  pl.BlockSpec((W, tc), lambda g, c: (0, 0))],
        out_specs=[pl.BlockSpec((bb, tc, W), lambda g, c: (g, c, 0)),
                   pl.BlockSpec((bb * n_hp, D_HEAD, LANES), lambda g, c: (g, 0, 0))],
        out_shape=[jax.ShapeDtypeStruct((B, Tp, W), F32),
                   jax.ShapeDtypeStruct((B * n_hp, D_HEAD, LANES), F32)],
        scratch_shapes=[pltpu.VMEM((bb * n_hp, D_HEAD, LANES), F32),
                        pltpu.VMEM((2 * bb * n_hp, D_HEAD, tc), F32)],
        compiler_params=_params("parallel", "arbitrary"),
        name="rwkv_scan",
    )(rs, gb, s0, lnw, lnb)


def _merge_kernel(ya_ref, yb_ref, yc_ref, gl_ref, x_ref, gt_ref, wb_ref, wo_ref, o_ref):
    D = x_ref.shape[2]
    mix = None
    for n, y_ref in enumerate((ya_ref, yb_ref, yc_ref)):
        proj = _dot_bf16(y_ref[0], wb_ref[n])
        gate = jax.nn.sigmoid(gl_ref[0, :, n * D:(n + 1) * D].astype(F32))
        mix = gate * proj if mix is None else mix + gate * proj
    o_ref[0] = x_ref[0] + gt_ref[0] * _dot_bf16(mix, wo_ref[...])


def merge(ya, yb, yc, gl, x, gt, wb, wo):
    B, T, D = x.shape
    W = ya.shape[2]
    tm = _row_tile(T, 256)
    tmod = tm if gt.shape[1] == T else 1
    mod_map = (lambda b, i: (b, i, 0)) if gt.shape[1] == T else (lambda b, i: (b, 0, 0))
    row = lambda c: pl.BlockSpec((1, tm, c), lambda b, i: (b, i, 0))
    return pl.pallas_call(
        _merge_kernel,
        grid=(B, T // tm),
        in_specs=[row(W), row(W), row(W), row(N_BRANCH * D), row(D), pl.BlockSpec((1, tmod, D), mod_map),
                  pl.BlockSpec(wb.shape, lambda b, i: (0, 0, 0)), pl.BlockSpec(wo.shape, lambda b, i: (0, 0))],
        out_specs=row(D),
        out_shape=jax.ShapeDtypeStruct((B, T, D), F32),
        compiler_params=_params("parallel", "parallel"),
        name="merge",
    )(ya, yb, yc, gl, x, gt, wb, wo)


def _top_values(x, n):
    vals = []
    for _ in range(n):
        m = jnp.max(x, axis=0, keepdims=True)
        vals.append(m)
        x = jnp.where(x == m, -jnp.inf, x)
    return vals


def _peer_route_kernel(x_ref, sc_ref, sh_ref, g_ref, wq_ref, keys_ref, ht_ref, s1_ref, s2_ref, e2_ref, st_ref):
    x = x_ref[0]
    h = x * lax.rsqrt(jnp.mean(x * x, axis=-1, keepdims=True) + EPS) * g_ref[...]
    h = h * (1.0 + sc_ref[0]) + sh_ref[0]
    ht_ref[...] = h.T.astype(BF16)
    q = _dot_bf16(h, wq_ref[...])
    tm = x.shape[0]
    for hd in range(PEER_HEADS):
        s1, s2 = (_dot_bf16(keys_ref[2 * hd + p], q[:, (2 * hd + p) * LANES:(2 * hd + p + 1) * LANES], _NT)
                  for p in range(2))
        t1 = _top_values(s1, PEER_TOPK)
        t2 = jnp.concatenate(_top_values(s2, PEER_TOPK), axis=0)
        best = _top_values(jnp.concatenate([a + t2 for a in t1], axis=0), PEER_TOPK)
        z = sum(jnp.exp(b - best[0]) for b in best)
        s1_ref[hd] = s1
        s2_ref[hd] = s2
        e2_ref[hd] = jnp.exp(s2 - t2[0:1])
        st_ref[hd] = jnp.concatenate([best[-1], t2[0:1] - best[0], 1.0 / z, jnp.zeros((SUBLANES - 3, tm), F32)],
                                     axis=0)


def peer_route(x, scale, shift, g, wq, keys):
    B, T, D = x.shape
    N = B * T
    tm = _row_tile(T, 256)
    nt = T // tm
    tmod = tm if scale.shape[1] == T else 1
    mod_map = (lambda b, i: (b, i, 0)) if scale.shape[1] == T else (lambda b, i: (b, 0, 0))
    hkn = jax.ShapeDtypeStruct((PEER_HEADS, PEER_KEYS, N), F32)
    hk_spec = pl.BlockSpec((PEER_HEADS, PEER_KEYS, tm), lambda b, i: (0, 0, b * nt + i))
    return pl.pallas_call(
        _peer_route_kernel,
        grid=(B, nt),
        in_specs=[pl.BlockSpec((1, tm, D), lambda b, i: (b, i, 0)),
                  pl.BlockSpec((1, tmod, D), mod_map), pl.BlockSpec((1, tmod, D), mod_map),
                  pl.BlockSpec((1, D), lambda b, i: (0, 0)),
                  pl.BlockSpec(wq.shape, lambda b, i: (0, 0)),
                  pl.BlockSpec(keys.shape, lambda b, i: (0, 0, 0))],
        out_specs=[pl.BlockSpec((D, tm), lambda b, i: (0, b * nt + i)), hk_spec, hk_spec, hk_spec,
                   pl.BlockSpec((PEER_HEADS, SUBLANES, tm), lambda b, i: (0, 0, b * nt + i))],
        out_shape=[jax.ShapeDtypeStruct((D, N), BF16), hkn, hkn, hkn,
                   jax.ShapeDtypeStruct((PEER_HEADS, SUBLANES, N), F32)],
        compiler_params=_params("parallel", "parallel"),
        name="peer_route",
    )(x, scale, shift, g.reshape(1, D), wq, keys)


def _peer_expert_kernel(ht_ref, s1_ref, s2_ref, e2_ref, st_ref, u_ref, vt_ref, x_ref, gt_ref, o_ref, acc_ref):
    e = pl.program_id(2)

    @pl.when(e == 0)
    def _():
        acc_ref[...] = jnp.zeros(acc_ref.shape, F32)

    act = jnp.dot(u_ref[...], ht_ref[...], preferred_element_type=F32)
    ws = []
    for r in range(u_ref.shape[0] // PEER_KEYS):
        gate = None
        for hd in range(PEER_HEADS):
            s1 = s1_ref[hd, r:r + 1, :]
            e1 = jnp.exp(s1 + st_ref[hd, 1:2, :]) * st_ref[hd, 2:3, :]
            gh = jnp.where(s2_ref[hd] + s1 >= st_ref[hd, 0:1, :], e2_ref[hd] * e1, 0.0)
            gate = gh if gate is None else gate + gh
        a = act[r * PEER_KEYS:(r + 1) * PEER_KEYS]
        gelu = 0.5 * a * (1.0 + lax.erf(a * (2.0 ** -0.5)))
        ws.append((gate * gelu).astype(BF16))
    acc_ref[...] += jnp.dot(vt_ref[...], jnp.concatenate(ws, axis=0), preferred_element_type=F32)

    @pl.when(e == pl.num_programs(2) - 1)
    def _():
        o_ref[0] = x_ref[0] + gt_ref[0] * acc_ref[...].T


def peer_expert(ht, s1, s2, e2, st, u, vt, x, gt):
    B, T, D = x.shape
    E = u.shape[0]
    tm = _row_tile(T, 256)
    nt = T // tm
    te = SUBLANES * PEER_KEYS
    tmod = tm if gt.shape[1] == T else 1
    mod_map = (lambda b, i, e: (b, i, 0)) if gt.shape[1] == T else (lambda b, i, e: (b, 0, 0))
    hk_spec = pl.BlockSpec((PEER_HEADS, PEER_KEYS, tm), lambda b, i, e: (0, 0, b * nt + i))
    return pl.pallas_call(
        _peer_expert_kernel,
        grid=(B, nt, E // te),
        in_specs=[pl.BlockSpec((D, tm), lambda b, i, e: (0, b * nt + i)),
                  pl.BlockSpec((PEER_HEADS, SUBLANES, tm), lambda b, i, e: (0, e, b * nt + i)),
                  hk_spec, hk_spec,
                  pl.BlockSpec((PEER_HEADS, SUBLANES, tm), lambda b, i, e: (0, 0, b * nt + i)),
                  pl.BlockSpec((te, D), lambda b, i, e: (e, 0)),
                  pl.BlockSpec((D, te), lambda b, i, e: (0, e)),
                  pl.BlockSpec((1, tm, D), lambda b, i, e: (b, i, 0)),
                  pl.BlockSpec((1, tmod, D), mod_map)],
        out_specs=pl.BlockSpec((1, tm, D), lambda b, i, e: (b, i, 0)),
        out_shape=jax.ShapeDtypeStruct((B, T, D), F32),
        scratch_shapes=[pltpu.VMEM((D, tm), F32)],
        compiler_params=_params("parallel", "parallel", "arbitrary"),
        name="peer_expert",
    )(ht, s1, s2, e2, st, u, vt, x, gt)


def _pack_state(s):
    B, H, dv, dk = s.shape
    return s.reshape(B, H // 2, 2, dv, dk).transpose(0, 1, 3, 2, 4).reshape(B * H // 2, dv, 2 * dk)


def _unpack_state(s, B):
    P, dv, dk2 = s.shape
    H = 2 * P // B
    return s.reshape(B, H // 2, dv, 2, dk2 // 2).transpose(0, 1, 3, 2, 4).reshape(B, H, dv, dk2 // 2)


def _block(x, ada, pos0, past, shift_prev, state, lw, rowwise_mods):
    B, T, D = x.shape
    N = B * T
    sh1, sc1, gt1, sh2, sc2, gt2 = ada
    if rowwise_mods:
        xr = x.reshape(1, N, D)
        mod = lambda m: jnp.repeat(m, T, axis=0).reshape(1, N, D)
    else:
        xr = x
        mod = lambda m: m.reshape(B, 1, D)
    h = modulate(xr, mod(sc1), mod(sh1), lw["norm_mix"], BF16).reshape(N, D)
    q3 = matmul(h, lw["w_q3"])
    mrow = matmul(h, lw["w_moba"])
    zr = matmul(h, lw["w_rw"], tn_want=lw["w_rw"].shape[1])
    drow = matmul(h, lw["w_dsa"])
    gl = matmul(h, lw["w_gl"], out_dtype=BF16)
    MR = mrow.shape[1]
    RW = zr.shape[1]
    q3b = q3.reshape(B, T, -1)
    drow_b = drow.reshape(B, T, -1)

    if past is None:
        ident = jnp.arange(N // PAGE_SIZE, dtype=jnp.int32).reshape(B, T // PAGE_SIZE)
        kv_a, km = paged_cast(mrow.reshape(1, N // PAGE_SIZE, PAGE_SIZE, MR), 0, ident, None, MR // 2)
        kv_c = paged_cast(drow.reshape(1, N // PAGE_SIZE, PAGE_SIZE, drow.shape[1]), 0, ident, None)
    else:
        pages_a, pages_c, layer, table = past
        pad = lambda r: jnp.pad(r, ((0, 0), (0, MOBA_BLOCK - T), (0, 0)))
        kv_a, km = paged_cast(pages_a, layer, table, pad(mrow.reshape(B, T, MR)), MR // 2)
        kv_c = paged_cast(pages_c, layer, table, pad(drow_b[:, :, :DSA_ROW]))
    nbp = -(-km.shape[1] // 16) * 16
    km = jnp.pad(km, ((0, 0), (0, nbp - km.shape[1]), (0, 0)))
    ya = moba_attention(q3b, kv_a, km, pos0)
    yc = dsa_attention(q3b, drow_b, kv_c, pos0, pos0 + T)

    zr_b = zr.reshape(B, T, RW)
    zs = jnp.concatenate([shift_prev[:, None], zr_b[:, :-1]], axis=1).reshape(N, RW)
    rs, gb = rwkv_prep(zr, zs, lw["rwkv_mu"], lw["rwkv_lora"], lw["rwkv_vecs"], lw["ones_blk"])
    Tp = -(-T // LANES) * LANES
    padt = lambda a: jnp.pad(a.reshape(B, T, -1), ((0, 0), (0, Tp - T), (0, 0)))
    A = gb.shape[1] // 2
    rs_p = padt(rs)
    if Tp > T:
        rs_p = rs_p.at[:, T:, A:2 * A].set(1.0)
    yb, s_new = rwkv_scan(rs_p, padt(gb), _pack_state(state), lw["rwkv_ln_w"], lw["rwkv_ln_b"], T)
    yb = yb[:, :T]

    W = ya.shape[2]
    rsh = (lambda a: a.reshape(1, N, -1)) if rowwise_mods else (lambda a: a.reshape(B, T, -1))
    x1 = merge(rsh(ya), rsh(yb), rsh(yc), rsh(gl), xr, mod(gt1), lw["w_branch"], lw["w_out"])
    ht, s1, s2, e2, st = peer_route(x1, mod(sc2), mod(sh2), lw["norm_ffn"], lw["peer_wq"], lw["peer_keys"])
    x2 = peer_expert(ht, s1, s2, e2, st, lw["peer_u"], lw["peer_vt"], x1, mod(gt2))
    return (x2.reshape(B, T, D), mrow.reshape(B, T, MR), drow_b[:, :, :DSA_ROW],
            _unpack_state(s_new, B), zr_b[:, -1])


def kernel(x_prompt, x_sample, cache_moba, cache_dsa, state_rwkv, state_shift, page_table, c_prompt, c_sample, w_ada, b_ada, norm_mix, w_in, rwkv_mu, rwkv_w0, rwkv_w_up, rwkv_a0, rwkv_a_up, rwkv_k_k, rwkv_k_a, rwkv_r_k, rwkv_ln_w, rwkv_ln_b, w_branch, w_out, norm_ffn, peer_wq, peer_keys, peer_u, peer_v, norm_final):
    n_b, n_t, D = x_prompt.shape
    n_db, n_dt, _ = x_sample.shape
    depth = w_in.shape[0]
    past_len = page_table.shape[1] * PAGE_SIZE
    A = D // 2
    RW = 4 * A + 2 * B_LORA
    idx_w = IDX_HEADS * IDX_DIM
    sizes = (A, 2 * A, RW, A, idx_w, IDX_HEADS, DSA_ROW, N_BRANCH * D)
    off = np.concatenate([[0], np.cumsum(sizes)])
    assert off[-1] == w_in.shape[2]

    n_c = n_b + n_db
    n_cp = -(-n_c // SUBLANES) * SUBLANES
    c_all = jnp.pad(jnp.concatenate([c_prompt, c_sample], axis=0), ((0, n_cp - n_c), (0, 0)))
    ada_all = ada_matmul(c_all, w_ada, b_ada)

    ones_blk = jnp.asarray(np.kron(np.eye(A // D_HEAD), np.ones((D_HEAD, D_HEAD))), BF16)
    xp, xs = x_prompt, x_sample
    outs = [[] for _ in range(8)]
    for l in range(depth):
        wl = w_in[l]
        seg = lambda n: wl[:, off[n]:off[n + 1]]
        dsa_pad = LANES * (-(-(DSA_ROW + IDX_HEADS) // LANES)) - DSA_ROW - IDX_HEADS
        lora = jnp.zeros((2 * B_LORA, 2 * A), F32)
        lora = lora.at[:B_LORA, :A].set(rwkv_w_up[l]).at[B_LORA:, A:].set(rwkv_a_up[l])
        flat = lambda a: a.reshape(1, -1)
        lw = dict(
            norm_mix=norm_mix[l], norm_ffn=norm_ffn[l],
            w_q3=jnp.concatenate([seg(0), seg(3), seg(4)], axis=1).astype(BF16),
            w_moba=seg(1).astype(BF16), w_rw=seg(2).astype(BF16),
            w_dsa=jnp.pad(jnp.concatenate([seg(6), seg(5)], axis=1), ((0, 0), (0, dsa_pad))).astype(BF16),
            w_gl=seg(7).astype(BF16),
            rwkv_mu=flat(rwkv_mu[l]), rwkv_lora=lora, ones_blk=ones_blk,
            rwkv_vecs=jnp.concatenate([flat(rwkv_w0[l]), flat(rwkv_a0[l]), flat(rwkv_k_k[l]), flat(rwkv_k_a[l]),
                                       flat(rwkv_r_k[l]), jnp.zeros((SUBLANES - 5, A), F32)], axis=0),
            rwkv_ln_w=rwkv_ln_w[l], rwkv_ln_b=rwkv_ln_b[l],
            w_branch=w_branch[l].astype(BF16), w_out=w_out[l].astype(BF16),
            peer_wq=peer_wq[l].astype(BF16),
            peer_keys=peer_keys[l].reshape(2 * PEER_HEADS, PEER_KEYS, -1).astype(BF16),
            peer_u=peer_u[l].astype(BF16), peer_vt=peer_v[l].T.astype(BF16),
        )
        ada_p = jnp.split(ada_all[l, :n_b], 6, axis=-1)
        ada_s = jnp.split(ada_all[l, n_b:n_c], 6, axis=-1)
        xp, m_p, d_p, r_p, s_p = _block(
            xp, ada_p, 0, None, jnp.zeros((n_b, RW), F32),
            jnp.zeros((n_b, A // D_HEAD, D_HEAD, D_HEAD), F32), lw, False)
        xs, m_s, d_s, r_s, s_s = _block(
            xs, ada_s, past_len, (cache_moba, cache_dsa, l, page_table), state_shift[l], state_rwkv[l], lw, True)
        for lst, val in zip(outs, (m_p, m_s, d_p, d_s, r_p, r_s, s_p, s_s)):
            lst.append(val)
    y_prompt = rms_norm_rows(xp.reshape(n_b * n_t, D), norm_final).reshape(n_b, n_t, D)
    y_sample = rms_norm_rows(xs.reshape(n_db * n_dt, D), norm_final).reshape(n_db, n_dt, D)
    return (y_prompt, y_sample) + tuple(jnp.stack(o) for o in outs)
```

```python
import functools

import jax
import jax.numpy as jnp
import numpy as np
from jax import lax
from jax.experimental import pallas as pl
from jax.experimental.pallas import tpu as pltpu

D_HEAD = 64
MOBA_BLOCK = 256
MOBA_TOPK = 3
B_LORA = 64
C_KV_HEADS = 2
IDX_HEADS = 8
IDX_DIM = 64
DSA_TOPK = 256
N_BRANCH = 3
PEER_KEYS = 128
PEER_HEADS = 8
PEER_TOPK = 16
PAGE_SIZE = 128
EPS = 1e-6
RWKV_GN_EPS = 64e-5
NEG = -1e30

LANES = 128
SUBLANES = 8
VMEM_LIMIT_BYTES = 56 * 1024 * 1024

F32 = jnp.float32
BF16 = jnp.bfloat16
INT_MIN = -2 ** 31


def _params(*sem):
    return pltpu.CompilerParams(dimension_semantics=sem, vmem_limit_bytes=VMEM_LIMIT_BYTES)


def _split_bf16(a):
    hi = a.astype(BF16)
    lo = (a - hi.astype(F32)).astype(BF16)
    return hi, lo


def _dot3(a, b, dims):
    ah, al = _split_bf16(a)
    bh, bl = _split_bf16(b)
    dn = (dims, ((), ()))
    d = lambda x, y: lax.dot_general(x, y, dn, preferred_element_type=F32)
    return d(ah, bh) + (d(ah, bl) + d(al, bh))


_NN = ((1,), (0,))
_NT = ((1,), (1,))


def _dot_bf16(a, b, dims=_NN):
    return lax.dot_general(a.astype(BF16), b.astype(BF16), (dims, ((), ())), preferred_element_type=F32)


def _row_tile(n, want):
    t = min(n, want)
    while n % t:
        t //= 2
    return t


def _modulate_kernel(x_ref, sc_ref, sh_ref, g_ref, o_ref):
    x = x_ref[0]
    y = x * lax.rsqrt(jnp.mean(x * x, axis=-1, keepdims=True) + EPS) * g_ref[...]
    o_ref[0] = (y * (1.0 + sc_ref[0]) + sh_ref[0]).astype(o_ref.dtype)


def modulate(x, scale, shift, g, out_dtype):
    B, T, D = x.shape
    tm = _row_tile(T, 512)
    tmod = tm if scale.shape[1] == T else 1
    mod_map = (lambda b, i: (b, i, 0)) if scale.shape[1] == T else (lambda b, i: (b, 0, 0))
    return pl.pallas_call(
        _modulate_kernel,
        grid=(B, T // tm),
        in_specs=[pl.BlockSpec((1, tm, D), lambda b, i: (b, i, 0)),
                  pl.BlockSpec((1, tmod, D), mod_map),
                  pl.BlockSpec((1, tmod, D), mod_map),
                  pl.BlockSpec((1, D), lambda b, i: (0, 0))],
        out_specs=pl.BlockSpec((1, tm, D), lambda b, i: (b, i, 0)),
        out_shape=jax.ShapeDtypeStruct((B, T, D), out_dtype),
        compiler_params=_params("parallel", "parallel"),
        name="modulate",
    )(x, scale, shift, g.reshape(1, D))


def _rmsnorm_kernel(x_ref, g_ref, o_ref):
    x = x_ref[...]
    o_ref[...] = x * lax.rsqrt(jnp.mean(x * x, axis=-1, keepdims=True) + EPS) * g_ref[...]


def rms_norm_rows(x, g):
    N, D = x.shape
    tm = _row_tile(N, 512)
    return pl.pallas_call(
        _rmsnorm_kernel,
        grid=(N // tm,),
        in_specs=[pl.BlockSpec((tm, D), lambda i: (i, 0)), pl.BlockSpec((1, D), lambda i: (0, 0))],
        out_specs=pl.BlockSpec((tm, D), lambda i: (i, 0)),
        out_shape=jax.ShapeDtypeStruct((N, D), F32),
        compiler_params=_params("parallel"),
        name="final_norm",
    )(x, g.reshape(1, D))


def _mm_kernel(a_ref, w_ref, o_ref):
    o_ref[...] = jnp.dot(a_ref[...], w_ref[...], preferred_element_type=F32).astype(o_ref.dtype)


def _mm3_kernel(a_ref, w_ref, b_ref, o_ref):
    o_ref[0] = _dot3(a_ref[...], w_ref[0], _NN) + b_ref[0]


def matmul(a, w, out_dtype=F32, tn_want=1024):
    N, K = a.shape
    C = w.shape[1]
    tm = _row_tile(N, 512)
    tn = _row_tile(C, tn_want)
    return pl.pallas_call(
        _mm_kernel,
        grid=(C // tn, N // tm),
        in_specs=[pl.BlockSpec((tm, K), lambda j, i: (i, 0)), pl.BlockSpec((K, tn), lambda j, i: (0, j))],
        out_specs=pl.BlockSpec((tm, tn), lambda j, i: (i, j)),
        out_shape=jax.ShapeDtypeStruct((N, C), out_dtype),
        compiler_params=_params("parallel", "parallel"),
        name="matmul",
    )(a, w)


def ada_matmul(c, w, b):
    M, D = c.shape
    L, _, C = w.shape
    tn = _row_tile(C, 1536)
    return pl.pallas_call(
        _mm3_kernel,
        grid=(L, C // tn),
        in_specs=[pl.BlockSpec((M, D), lambda l, j: (0, 0)),
                  pl.BlockSpec((1, D, tn), lambda l, j: (l, 0, j)),
                  pl.BlockSpec((1, 1, tn), lambda l, j: (l, 0, j))],
        out_specs=pl.BlockSpec((1, M, tn), lambda l, j: (l, 0, j)),
        out_shape=jax.ShapeDtypeStruct((L, M, C), F32),
        compiler_params=_params("parallel", "parallel"),
        name="ada_matmul",
    )(c, w, b.reshape(L, 1, C))


def _paged_cast_kernel(tbl_ref, p0_ref, p1_ref, tail_ref, o_ref, *mean_ref, n_steps, mean_cols):
    j = pl.program_id(1)

    def emit(rows):
        o_ref[0] = rows.astype(o_ref.dtype)
        if mean_ref:
            mean_ref[0][0] = jnp.mean(rows[:, :mean_cols], axis=0, keepdims=True)

    @pl.when(j < n_steps)
    def _():
        emit(jnp.concatenate([p0_ref[0, 0], p1_ref[0, 0]], axis=0))

    @pl.when(j >= n_steps)
    def _():
        emit(tail_ref[0])


def paged_cast(pages, layer, table, tail, mean_cols=0):
    _, _, page, R = pages.shape
    B, n_pg = table.shape
    assert page == PAGE_SIZE and MOBA_BLOCK == 2 * PAGE_SIZE and n_pg % 2 == 0
    n_steps = n_pg // 2
    has_tail = tail is not None
    n_blk = n_steps + (1 if has_tail else 0)
    if not has_tail:
        tail = jnp.zeros((B, MOBA_BLOCK, R), F32)
    last = n_steps - 1

    def page_map(k):
        return lambda b, j, tbl: (layer, tbl[b, 2 * jnp.minimum(j, last) + k], 0, 0)

    out_shape = [jax.ShapeDtypeStruct((B, n_blk * MOBA_BLOCK, R), BF16)]
    out_specs = [pl.BlockSpec((1, MOBA_BLOCK, R), lambda b, j, tbl: (b, j, 0))]
    if mean_cols:
        out_shape.append(jax.ShapeDtypeStruct((B * n_blk, 1, mean_cols), F32))
        out_specs.append(pl.BlockSpec((1, 1, mean_cols), lambda b, j, tbl: (b * n_blk + j, 0, 0)))
    res = pl.pallas_call(
        functools.partial(_paged_cast_kernel, n_steps=n_steps, mean_cols=mean_cols),
        grid_spec=pltpu.PrefetchScalarGridSpec(
            num_scalar_prefetch=1,
            grid=(B, n_blk),
            in_specs=[pl.BlockSpec((1, 1, PAGE_SIZE, R), page_map(0)),
                      pl.BlockSpec((1, 1, PAGE_SIZE, R), page_map(1)),
                      pl.BlockSpec((1, MOBA_BLOCK, R), lambda b, j, tbl: (b, 0, 0))],
            out_specs=out_specs),
        out_shape=out_shape,
        compiler_params=_params("parallel", "arbitrary"),
        name="paged_cast",
    )(table, pages, pages, tail)
    if mean_cols:
        return res[0], res[1].reshape(B, n_blk, mean_cols)
    return res[0]


CAUSAL_CLASSES = 4


def _prefix_lengths(L):
    n = CAUSAL_CLASSES
    while L % (n * MOBA_BLOCK):
        n //= 2
    return [L * (c + 1) // n for c in range(n)]


def _for_visible_prefix(L, last_pos, body):
    prev = 0
    for lc in _prefix_lengths(L):
        pl.when((last_pos >= prev) & (last_pos < lc))(functools.partial(body, lc))
        prev = lc


def _moba_kernel(q_ref, k_ref, v_ref, km_ref, o_ref, *, pos0, tq):
    i = pl.program_id(2)
    body = functools.partial(_moba_body, q_ref, k_ref, v_ref, km_ref, o_ref, pos0 + i * tq, tq)
    _for_visible_prefix(k_ref.shape[1], pos0 + (i + 1) * tq - 1, body)


def _moba_body(q_ref, k_ref, v_ref, km_ref, o_ref, q_pos0, tq, L):
    nbp = km_ref.shape[1]
    q = q_ref[0] * (D_HEAD ** -0.5)
    k = k_ref[0, :L]
    v = v_ref[0, :L]
    km = km_ref[0]
    lane = lax.broadcasted_iota(jnp.int32, (1, LANES), 1)
    q_pos = q_pos0 + lax.broadcasted_iota(jnp.int32, (tq, 1), 0)
    own = q_pos // MOBA_BLOCK
    key_pos = lax.broadcasted_iota(jnp.int32, (1, L), 1)
    causal = jnp.where(key_pos > q_pos, NEG, 0.0)
    blk_id = lax.broadcasted_iota(jnp.int32, (tq, nbp), 1).astype(F32)
    own = own.astype(F32)
    expand =jnp.where(lax.broadcasted_iota(jnp.int32, (nbp, L), 1) // MOBA_BLOCK
                       == lax.broadcasted_iota(jnp.int32, (nbp, L), 0), 1.0, 0.0).astype(BF16)
    past = blk_id < own
    outs = []
    for h in range(2):
        qh = jnp.where(lane // D_HEAD == h, q, 0.0)
        g = jnp.where(past, _dot3(qh, km, _NT), NEG)
        closed = jnp.where(blk_id == own, 0.0, NEG)
        for _ in range(MOBA_TOPK):
            m = jnp.max(g, axis=-1, keepdims=True)
            first = jnp.min(jnp.where(g == m, blk_id, nbp), axis=-1, keepdims=True)
            pick = blk_id == first
            closed = jnp.where(pick & past, 0.0, closed)
            g = jnp.where(pick, -jnp.inf, g)
        s = _dot_bf16(qh, k, _NT) + _dot_bf16(closed, expand) + causal
        p = jnp.exp(s - jnp.max(s, axis=-1, keepdims=True))
        o = _dot_bf16(p, v) / jnp.sum(p, axis=-1, keepdims=True)
        outs.append(o)
    o_ref[0] = jnp.where(lane // D_HEAD == 0, outs[0], outs[1])


def moba_attention(q3, kv, km, pos0):
    B, T, _ = q3.shape
    L = kv.shape[1]
    nbp = km.shape[1]
    tq = _row_tile(T, 128)
    n_pair = kv.shape[2] // (2 * LANES)
    return pl.pallas_call(
        functools.partial(_moba_kernel, pos0=pos0, tq=tq),
        grid=(B, n_pair, T // tq),
        in_specs=[pl.BlockSpec((1, tq, LANES), lambda b, hp, i: (b, i, hp)),
                  pl.BlockSpec((1, L, LANES), lambda b, hp, i: (b, 0, hp)),
                  pl.BlockSpec((1, L, LANES), lambda b, hp, i: (b, 0, n_pair + hp)),
                  pl.BlockSpec((1, nbp, LANES), lambda b, hp, i: (b, 0, hp))],
        out_specs=pl.BlockSpec((1, tq, LANES), lambda b, hp, i: (b, i, hp)),
        out_shape=jax.ShapeDtypeStruct((B, T, n_pair * LANES), F32),
        compiler_params=_params("parallel", "parallel", "parallel"),
        name="moba_attention",
    )(q3, kv, kv, km)


DSA_ROW = 2 * C_KV_HEADS * D_HEAD + IDX_DIM


def _dsa_kernel(qc_ref, qi_ref, wi_ref, kv_ref, o_ref, key_full, bias_full, *, pos0, tq, topk, n_heads):
    i = pl.program_id(1)
    body = functools.partial(_dsa_body, qc_ref, qi_ref, wi_ref, kv_ref, o_ref, key_full, bias_full,
                             pos0 + i * tq, tq, topk, n_heads)
    _for_visible_prefix(kv_ref.shape[1], pos0 + (i + 1) * tq - 1, body)


def _dsa_body(qc_ref, qi_ref, wi_ref, kv_ref, o_ref, key_full, bias_full, q_pos0, tq, topk, n_heads, L):
    key_scr = key_full.at[:, :L]
    bias_scr = bias_full.at[:, :L]
    kvw = C_KV_HEADS * D_HEAD
    k2 = kv_ref[0, :L, 0:kvw]
    v2 = kv_ref[0, :L, kvw:2 * kvw]
    ki = kv_ref[0, :L, 2 * kvw:2 * kvw + IDX_DIM]
    qc = qc_ref[0] * (D_HEAD ** -0.5)
    qi = qi_ref[0]
    wi = wi_ref[0][:, DSA_ROW:DSA_ROW + IDX_HEADS] * (IDX_HEADS ** -0.5)
    q_pos = q_pos0 + lax.broadcasted_iota(jnp.int32, (tq, 1), 0)
    key_pos = lax.broadcasted_iota(jnp.int32, (1, L), 1)
    valid = key_pos <= q_pos

    score = jnp.zeros((tq, L), F32)
    for h in range(IDX_HEADS):
        lg = _dot_bf16(qi[:, h * IDX_DIM:(h + 1) * IDX_DIM], ki, _NT)
        score = score + jnp.maximum(lg, 0.0) * wi[:, h:h + 1]
    score = jnp.where(valid, score, NEG)
    score = jnp.where(score == 0.0, 0.0, score)
    bits = pltpu.bitcast(score, jnp.int32)
    key = jnp.where(bits < 0, bits ^ 0x7FFFFFFF, bits)
    key_scr[...] = key

    def count_ge(c):
        return jnp.sum(jnp.where(key_scr[...] >= c, 1.0, 0.0), axis=-1, keepdims=True)

    t0 = jnp.where(count_ge(jnp.zeros((tq, 1), jnp.int32)) >= topk, 0, INT_MIN).astype(jnp.int32)

    def bit_step(it, t):
        c = t | lax.shift_left(jnp.int32(1), 30 - it)
        return jnp.where(count_ge(c) >= topk, c, t)

    thr = lax.fori_loop(0, 31, bit_step, t0)
    key = key_scr[...]
    take = (key >= thr) & valid
    bias_scr[...] = jnp.where(take, 0.0, NEG)
    n_take = jnp.sum(jnp.where(take, 1.0, 0.0), axis=-1, keepdims=True)

    @pl.when(jnp.max(n_take) > topk)
    def _():
        above = (key > thr) & valid
        need = topk - jnp.sum(jnp.where(above, 1.0, 0.0), axis=-1, keepdims=True)
        upper = jnp.where(lax.broadcasted_iota(jnp.int32, (LANES, LANES), 0)
                          < lax.broadcasted_iota(jnp.int32, (LANES, LANES), 1), 1.0, 0.0).astype(BF16)

        def blk(j, seen):
            sl = pl.ds(pl.multiple_of(j * LANES, LANES), LANES)
            kj = key_scr[:, sl]
            ok = (j * LANES + lax.broadcasted_iota(jnp.int32, (1, LANES), 1)) <= q_pos
            tied = jnp.where((kj == thr) & ok, 1.0, 0.0)
            before = seen + _dot_bf16(tied, upper)
            keep = ((kj > thr) & ok) | ((tied > 0.0) & (before < need))
            bias_scr[:, sl] = jnp.where(keep, 0.0, NEG)
            return seen + jnp.sum(tied, axis=-1, keepdims=True)

        lax.fori_loop(0, L // LANES, blk, jnp.zeros((tq, 1), F32))

    bias = bias_scr[...]
    grp = n_heads // C_KV_HEADS
    zero = jnp.zeros((tq, D_HEAD), F32)
    outs = []
    for h in range(n_heads):
        g = h // grp
        qh = qc[:, h * D_HEAD:(h + 1) * D_HEAD]
        q128 = jnp.concatenate([qh, zero] if g == 0 else [zero, qh], axis=1)
        s = _dot_bf16(q128, k2, _NT) + bias
        p = jnp.exp(s - jnp.max(s, axis=-1, keepdims=True))
        o = _dot_bf16(p, v2) / jnp.sum(p, axis=-1, keepdims=True)
        outs.append(o[:, g * D_HEAD:(g + 1) * D_HEAD])
    o_ref[0] = jnp.concatenate(outs, axis=1)


def dsa_attention(q3, drow, kv, pos0, n_keys):
    B, T, _ = q3.shape
    L, R = kv.shape[1], kv.shape[2]
    W = q3.shape[2] // 3
    tq = _row_tile(T, 128)
    topk = min(DSA_TOPK, n_keys // 4)
    return pl.pallas_call(
        functools.partial(_dsa_kernel, pos0=pos0, tq=tq, topk=topk, n_heads=W // D_HEAD),
        grid=(B, T // tq),
        in_specs=[pl.BlockSpec((1, tq, W), lambda b, i: (b, i, 1)),
                  pl.BlockSpec((1, tq, W), lambda b, i: (b, i, 2)),
                  pl.BlockSpec((1, tq, drow.shape[2]), lambda b, i: (b, i, 0)),
                  pl.BlockSpec((1, L, R), lambda b, i: (b, 0, 0))],
        out_specs=pl.BlockSpec((1, tq, W), lambda b, i: (b, i, 0)),
        out_shape=jax.ShapeDtypeStruct((B, T, W), F32),
        scratch_shapes=[pltpu.VMEM((tq, L), jnp.int32), pltpu.VMEM((tq, L), F32)],
        compiler_params=_params("parallel", "parallel"),
        name="dsa_attention",
    )(q3, q3, drow, kv)


def _segsum(x, ones_blk):
    hi = x.astype(BF16)
    r1 = x - hi.astype(F32)
    mid = r1.astype(BF16)
    lo = (r1 - mid.astype(F32)).astype(BF16)
    d = lambda a: jnp.dot(a, ones_blk, preferred_element_type=F32)
    return d(hi) + (d(mid) + d(lo))


def _rwkv_prep_kernel(z_ref, zs_ref, mu_ref, lora_ref, vec_ref, ones_ref, rs_ref, gb_ref, *t3_ref, width):
    W = width
    z = z_ref[0]
    xm = z + (zs_ref[0] - z) * mu_ref[...]
    r, k, v, g = (xm[:, n * W:(n + 1) * W] for n in range(4))
    lat = xm[:, 4 * W:4 * W + 2 * B_LORA]
    lane = lax.broadcasted_iota(jnp.int32, (1, 2 * B_LORA), 1)
    lat = jnp.where(lane < B_LORA, jnp.tanh(lat), lat)
    up = _dot3(lat, lora_ref[...], _NN)
    w0, a0, k_k, k_a, r_k = (vec_ref[n:n + 1, :] for n in range(5))
    wpre = -(w0 + up[:, :W])
    softplus = jnp.maximum(wpre, 0.0) + jnp.log(1.0 + jnp.exp(-jnp.abs(wpre)))
    log_decay = -jnp.exp(-softplus - 0.5)
    a = jax.nn.sigmoid(a0 + up[:, W:])
    ones_blk = ones_ref[...]
    kk = k * k_k
    kk = kk / jnp.maximum(jnp.sqrt(_segsum(kk * kk, ones_blk)), 1e-12)
    k = k * (1.0 + (a - 1.0) * k_a)
    kka = kk * a
    bonus = _segsum(r * k * r_k, ones_blk) * v
    rs_ref[0] = jnp.concatenate([r, jnp.exp(log_decay), k, kk, kka, v, log_decay], axis=1)
    gb_ref[0] = jnp.concatenate([jax.nn.sigmoid(g), bonus], axis=1)
    if t3_ref:
        t3_ref[0][0] = jnp.concatenate([log_decay.T, kka.T, k.T], axis=0)


RS_SEGMENTS = 7


def rwkv_prep(zr, zs, mu, lora, vecs, ones_blk, key_major):
    B, T, C = zr.shape
    W = vecs.shape[1]
    tm = _row_tile(T, 256)
    full = lambda a: pl.BlockSpec(a.shape, lambda b, i: (0,) * a.ndim)
    row = lambda c: pl.BlockSpec((1, tm, c), lambda b, i: (b, i, 0))
    out_specs = [row(RS_SEGMENTS * W), row(2 * W)]
    out_shape = [jax.ShapeDtypeStruct((B, T, RS_SEGMENTS * W), F32), jax.ShapeDtypeStruct((B, T, 2 * W), F32)]
    if key_major:
        out_specs.append(pl.BlockSpec((1, 3 * W, tm), lambda b, i: (b, 0, i)))
        out_shape.append(jax.ShapeDtypeStruct((B, 3 * W, T), F32))
    return pl.pallas_call(
        functools.partial(_rwkv_prep_kernel, width=W),
        grid=(B, T // tm),
        in_specs=[row(C), row(C), full(mu), full(lora), full(vecs), full(ones_blk)],
        out_specs=out_specs,
        out_shape=out_shape,
        compiler_params=_params("parallel", "parallel"),
        name="rwkv_prep",
    )(zr, zs, mu, lora, vecs, ones_blk)


RWKV_CHUNK = 64


def _split(a):
    return _split_bf16(a)


def _dot3s(a, b, dims=_NN):
    dn = (dims, ((), ()))
    d = lambda x, y: lax.dot_general(x, y, dn, preferred_element_type=F32)
    return d(a[0], b[0]) + (d(a[0], b[1]) + d(a[1], b[0]))


def _rows(s, lo, hi):
    return s[0][lo:hi], s[1][lo:hi]


def _rwkv_chunk_kernel(rs_ref, t3_ref, gb_ref, s0_ref, ln_ref, y_ref, sn_ref, st_scr, *, bb, width):
    c = pl.program_id(1)
    W = width
    n_hp = W // LANES
    C = RWKV_CHUNK
    TL = 2 * C
    assert TL == LANES and C == D_HEAD

    @pl.when(c == 0)
    def _():
        st_scr[...] = s0_ref[...]

    ri = lax.broadcasted_iota(jnp.int32, (TL, TL), 0)
    ci = lax.broadcasted_iota(jnp.int32, (TL, TL), 1)
    same = (ri // C) == (ci // C)
    strict = same & (ci < ri)
    incl = same & (ci <= ri)
    low_blk = jnp.where(incl, 1.0, 0.0).astype(BF16)
    up_blk = jnp.where(same & (ri <= ci), 1.0, 0.0).astype(BF16)
    ones_blk = jnp.where(same, 1.0, 0.0).astype(BF16)
    eye = jnp.where(ri == ci, 1.0, 0.0)
    lane = lax.broadcasted_iota(jnp.int32, (1, TL), 1)
    left = lane < C
    zeros_c = jnp.zeros((C, TL), F32)

    def exact3(x, w, x_first):
        hi = x.astype(BF16)
        r1 = x - hi.astype(F32)
        mid = r1.astype(BF16)
        lo = (r1 - mid.astype(F32)).astype(BF16)
        d = (lambda a: jnp.dot(a, w, preferred_element_type=F32)) if x_first else \
            (lambda a: jnp.dot(w, a, preferred_element_type=F32))
        return d(hi) + (d(mid) + d(lo))

    for p in range(bb * n_hp):
        b, hp = divmod(p, n_hp)
        seg = lambda n: rs_ref[b, :, pl.ds(n * W + hp * LANES, LANES)]
        r, k, kk, kka, v, lw = seg(0), seg(2), seg(3), seg(4), seg(5), seg(6)
        lw_t, kka_t, k_t = (t3_ref[b, pl.ds(n * W + hp * LANES, LANES), :] for n in range(3))
        cum = exact3(lw, low_blk, False)
        cum_t = exact3(lw_t, up_blk, True)
        a_til = -kk * jnp.exp(cum - lw)
        r_til = r * jnp.exp(cum)
        inv = jnp.exp(-cum)
        bk = _split(jnp.concatenate([kka * inv, k * inv], axis=0))
        cend = jnp.where(left, cum_t[:, C - 1:C], cum_t[:, TL - 1:TL])
        rest = jnp.exp(cend - cum_t)
        bk_hat_t = jnp.concatenate([kka_t * rest, k_t * rest], axis=1)
        vs = _split(v)

        lak, mrbk, tinv = [], [], []
        for h in range(2):
            mh = (lane // C) == h
            ar = _split(jnp.concatenate([jnp.where(mh, a_til, 0.0), jnp.where(mh, r_til, 0.0)], axis=0))
            m = _dot3s(ar, bk, _NT)
            lab = jnp.where(strict, m[:TL, :TL], 0.0)
            lak.append(_split(jnp.where(strict, m[:TL, TL:], 0.0)))
            mrbk.append(_split(jnp.concatenate([jnp.where(incl, m[TL:, :TL], 0.0),
                                                jnp.where(incl, m[TL:, TL:], 0.0)], axis=1)))
            x = _split(lab)
            t = eye + lab
            for _ in range(5):
                xx = _dot3s(x, x)
                x = _split(xx)
                t = t + _dot3s(x, _split(t))
            tinv.append(_split(t))

        st = st_scr[p]
        u_ext = None
        ys = []
        for s in range(2):
            lo, hi = s * C, (s + 1) * C
            sts = _split(st)
            ars = _dot3s(_split(jnp.concatenate([a_til[lo:hi], r_til[lo:hi]], axis=0)), sts)
            rhs = ars[:C] + jnp.where(left, _dot3s(_rows(lak[0], lo, hi), vs), _dot3s(_rows(lak[1], lo, hi), vs))
            rhs_ext = _split(jnp.concatenate([rhs, zeros_c] if s == 0 else [zeros_c, rhs], axis=0))
            u = jnp.where(left, _dot3s(_rows(tinv[0], lo, hi), rhs_ext), _dot3s(_rows(tinv[1], lo, hi), rhs_ext))
            u_ext = jnp.concatenate([u, zeros_c] if s == 0 else [u_ext[:C], u], axis=0)
            uv = _split(jnp.concatenate([u_ext, v], axis=0))
            ys.append(ars[C:] + jnp.where(left, _dot3s(_rows(mrbk[0], lo, hi), uv),
                                          _dot3s(_rows(mrbk[1], lo, hi), uv)))
            in_chunk = (lax.broadcasted_iota(jnp.int32, (1, 2 * TL), 1) % TL) // C == s
            upd = _dot3s(_split(jnp.where(in_chunk, bk_hat_t, 0.0)), uv)
            st = jnp.where(same, jnp.exp(cum_t[:, hi - 1:hi]) * st + upd, 0.0)
        st_scr[p] = st

        y = jnp.concatenate(ys, axis=0)
        mean = exact3(y, ones_blk, True) * (1.0 / C)
        var = exact3(jnp.square(y - mean), ones_blk, True) * (1.0 / C)
        cols = pl.ds(hp * LANES, LANES)
        yn = (y - mean) * lax.rsqrt(var + RWKV_GN_EPS) * ln_ref[0:1, cols] + ln_ref[1:2, cols]
        y_ref[b, :, cols] = (yn + gb_ref[b, :, pl.ds(W + hp * LANES, LANES)]) * gb_ref[b, :, cols]

    @pl.when(c == pl.num_programs(1) - 1)
    def _():
        sn_ref[...] = st_scr[...]


def rwkv_chunk_scan(rs, t3, gb, s0, ln_w, ln_b):
    B, T, _ = rs.shape
    W = gb.shape[2] // 2
    n_hp = W // LANES
    TL = 2 * RWKV_CHUNK
    bb = 2 if B % 2 == 0 else 1
    assert T % TL == 0
    ln = jnp.concatenate([ln_w.reshape(1, W), ln_b.reshape(1, W), jnp.zeros((SUBLANES - 2, W), F32)], axis=0)
    st_spec = pl.BlockSpec((bb * n_hp, LANES, LANES), lambda g, c: (g, 0, 0))
    return pl.pallas_call(
        functools.partial(_rwkv_chunk_kernel, bb=bb, width=W),
        grid=(B // bb, T // TL),
        in_specs=[pl.BlockSpec((bb, TL, RS_SEGMENTS * W), lambda g, c: (g, c, 0)),
                  pl.BlockSpec((bb, 3 * W, TL), lambda g, c: (g, 0, c)),
                  pl.BlockSpec((bb, TL, 2 * W), lambda g, c: (g, c, 0)),
                  st_spec,
                  pl.BlockSpec((SUBLANES, W), lambda g, c: (0, 0))],
        out_specs=[pl.BlockSpec((bb, TL, W), lambda g, c: (g, c, 0)), st_spec],
        out_shape=[jax.ShapeDtypeStruct((B, T, W), F32),
                   jax.ShapeDtypeStruct((B * n_hp, LANES, LANES), F32)],
        scratch_shapes=[pltpu.VMEM((bb * n_hp, LANES, LANES), F32)],
        compiler_params=_params("parallel", "arbitrary"),
        name="rwkv_chunk_scan",
    )(rs, t3, gb, s0, ln)


def _rwkv_scan_kernel(rs_ref, gb_ref, s0_ref, lnw_ref, lnb_ref, y_ref, sn_ref, s_scr, yt_scr,
                      *, bb, tc, t_total, width):
    c = pl.program_id(1)
    n_hp = width // LANES
    n_pair = bb * n_hp

    @pl.when(c == 0)
    def _():
        s_scr[...] = s0_ref[...]

    yt_scr[...] = jnp.zeros(yt_scr.shape, F32)
    lane = lax.broadcasted_iota(jnp.int32, (1, LANES), 1)
    left = lane < D_HEAD
    eye2 = jnp.where(lax.broadcasted_iota(jnp.int32, (D_HEAD, LANES), 0)
                     == lax.broadcasted_iota(jnp.int32, (D_HEAD, LANES), 1) % D_HEAD, 1.0, 0.0)
    tok = lax.broadcasted_iota(jnp.int32, (1, tc), 1)

    def halves(p):
        return (jnp.sum(jnp.where(left, p, 0.0), axis=-1, keepdims=True),
                jnp.sum(jnp.where(left, 0.0, p), axis=-1, keepdims=True))

    def step(g, carry):
        t0 = pl.multiple_of(g * SUBLANES, SUBLANES)
        for p in range(n_pair):
            b, hp = divmod(p, n_hp)
            grp = [rs_ref[b, pl.ds(t0, SUBLANES), pl.ds(n * width + hp * LANES, LANES)] for n in range(6)]
            s = s_scr[p]
            y_l, y_r = yt_scr[2 * p], yt_scr[2 * p + 1]
            for j in range(SUBLANES):
                r, w, k, kk, kka, v = (a[j:j + 1] for a in grp)
                sa_l, sa_r = halves(s * kk)
                v_l, v_r = halves(eye2 * v)
                s = s * w - jnp.where(left, sa_l, sa_r) * kka + jnp.where(left, v_l, v_r) * k
                o_l, o_r = halves(s * r)
                here = tok == t0 + j
                y_l = jnp.where(here, o_l, y_l)
                y_r = jnp.where(here, o_r, y_r)
            s_scr[p] = s
            yt_scr[2 * p], yt_scr[2 * p + 1] = y_l, y_r
        return carry

    n_tok = jnp.minimum(tc, t_total - c * tc)
    lax.fori_loop(0, (n_tok + SUBLANES - 1) // SUBLANES, step, 0)

    for p in range(n_pair):
        b, hp = divmod(p, n_hp)
        rows = []
        for h2 in range(2):
            y = yt_scr[2 * p + h2]
            mean = jnp.mean(y, axis=0, keepdims=True)
            var = jnp.mean(jnp.square(y - mean), axis=0, keepdims=True)
            seg = pl.ds(hp * LANES + h2 * D_HEAD, D_HEAD)
            rows.append((y - mean) * lax.rsqrt(var + RWKV_GN_EPS) * lnw_ref[seg, :] + lnb_ref[seg, :])
        yn = jnp.concatenate(rows, axis=0).T
        cols = pl.ds(hp * LANES, LANES)
        y_ref[b, :, cols] = (yn + gb_ref[b, :, pl.ds(width + hp * LANES, LANES)]) * gb_ref[b, :, cols]

    @pl.when(c == pl.num_programs(1) - 1)
    def _():
        sn_ref[...] = s_scr[...]


def rwkv_scan(rs, gb, s0, ln_w, ln_b, t_total):
    B, Tp, _ = rs.shape
    W = gb.shape[2] // 2
    n_hp = W // LANES
    tc = LANES
    bb = min(B, 4)
    assert B % bb == 0 and Tp % tc == 0
    lnw = jnp.broadcast_to(ln_w.reshape(W, 1), (W, tc))
    lnb = jnp.broadcast_to(ln_b.reshape(W, 1), (W, tc))
    return pl.pallas_call(
        functools.partial(_rwkv_scan_kernel, bb=bb, tc=tc, t_total=t_total, width=W),
        grid=(B // bb, Tp // tc),
        in_specs=[pl.BlockSpec((bb, tc, RS_SEGMENTS * W), lambda g, c: (g, c, 0)),
                  pl.BlockSpec((bb, tc, 2 * W), lambda g, c: (g, c, 0)),
                  pl.BlockSpec((bb * n_hp, D_HEAD, LANES), lambda g, c: (g, 0, 0)),
                  pl.BlockSpec((W, tc), lambda g, c: (0, 0)),
                  pl.BlockSpec((W, tc), lambda g, c: (0, 0))],
        out_specs=[pl.BlockSpec((bb, tc, W), lambda g, c: (g, c, 0)),
                   pl.BlockSpec((bb * n_hp, D_HEAD, LANES), lambda g, c: (g, 0, 0))],
        out_shape=[jax.ShapeDtypeStruct((B, Tp, W), F32),
                   jax.ShapeDtypeStruct((B * n_hp, D_HEAD, LANES), F32)],
        scratch_shapes=[pltpu.VMEM((bb * n_hp, D_HEAD, LANES), F32),
                        pltpu.VMEM((2 * bb * n_hp, D_HEAD, tc), F32)],
        compiler_params=_params("parallel", "arbitrary"),
        name="rwkv_scan",
    )(rs, gb, s0, lnw, lnb)


def _merge_kernel(ya_ref, yb_ref, yc_ref, gl_ref, x_ref, gt_ref, wb_ref, wo_ref, o_ref):
    D = x_ref.shape[2]
    mix = None
    for n, y_ref in enumerate((ya_ref, yb_ref, yc_ref)):
        proj = _dot_bf16(y_ref[0], wb_ref[n])
        gate = jax.nn.sigmoid(gl_ref[0, :, n * D:(n + 1) * D].astype(F32))
        mix = gate * proj if mix is None else mix + gate * proj
    o_ref[0] = x_ref[0] + gt_ref[0] * _dot_bf16(mix, wo_ref[...])


def merge(ya, yb, yc, gl, x, gt, wb, wo):
    B, T, D = x.shape
    W = ya.shape[2]
    tm = _row_tile(T, 256)
    tmod = tm if gt.shape[1] == T else 1
    mod_map = (lambda b, i: (b, i, 0)) if gt.shape[1] == T else (lambda b, i: (b, 0, 0))
    row = lambda c: pl.BlockSpec((1, tm, c), lambda b, i: (b, i, 0))
    return pl.pallas_call(
        _merge_kernel,
        grid=(B, T // tm),
        in_specs=[row(W), row(W), row(W), row(N_BRANCH * D), row(D), pl.BlockSpec((1, tmod, D), mod_map),
                  pl.BlockSpec(wb.shape, lambda b, i: (0, 0, 0)), pl.BlockSpec(wo.shape, lambda b, i: (0, 0))],
        out_specs=row(D),
        out_shape=jax.ShapeDtypeStruct((B, T, D), F32),
        compiler_params=_params("parallel", "parallel"),
        name="merge",
    )(ya, yb, yc, gl, x, gt, wb, wo)


def _top_values(x, n):
    vals = []
    for _ in range(n):
        m = jnp.max(x, axis=0, keepdims=True)
        vals.append(m)
        x = jnp.where(x == m, -jnp.inf, x)
    return vals


def _peer_route_kernel(x_ref, sc_ref, sh_ref, g_ref, wq_ref, keys_ref, ht_ref, s1_ref, s2_ref, e2_ref, st_ref):
    x = x_ref[0]
    h = x * lax.rsqrt(jnp.mean(x * x, axis=-1, keepdims=True) + EPS) * g_ref[...]
    h = h * (1.0 + sc_ref[0]) + sh_ref[0]
    ht_ref[...] = h.T.astype(BF16)
    q = _dot_bf16(h, wq_ref[...])
    tm = x.shape[0]
    for hd in range(PEER_HEADS):
        s1, s2 = (_dot_bf16(keys_ref[2 * hd + p], q[:, (2 * hd + p) * LANES:(2 * hd + p + 1) * LANES], _NT)
                  for p in range(2))
        t1 = _top_values(s1, PEER_TOPK)
        t2 = jnp.concatenate(_top_values(s2, PEER_TOPK), axis=0)
        best = _top_values(jnp.concatenate([a + t2 for a in t1], axis=0), PEER_TOPK)
        z = sum(jnp.exp(b - best[0]) for b in best)
        s1_ref[hd] = s1
        s2_ref[hd] = s2
        e2_ref[hd] = jnp.exp(s2 - t2[0:1])
        st_ref[hd] = jnp.concatenate([best[-1], t2[0:1] - best[0], 1.0 / z, jnp.zeros((SUBLANES - 3, tm), F32)],
                                     axis=0)


def peer_route(x, scale, shift, g, wq, keys):
    B, T, D = x.shape
    N = B * T
    tm = _row_tile(T, 256)
    nt = T // tm
    tmod = tm if scale.shape[1] == T else 1
    mod_map = (lambda b, i: (b, i, 0)) if scale.shape[1] == T else (lambda b, i: (b, 0, 0))
    hkn = jax.ShapeDtypeStruct((PEER_HEADS, PEER_KEYS, N), F32)
    hk_spec = pl.BlockSpec((PEER_HEADS, PEER_KEYS, tm), lambda b, i: (0, 0, b * nt + i))
    return pl.pallas_call(
        _peer_route_kernel,
        grid=(B, nt),
        in_specs=[pl.BlockSpec((1, tm, D), lambda b, i: (b, i, 0)),
                  pl.BlockSpec((1, tmod, D), mod_map), pl.BlockSpec((1, tmod, D), mod_map),
                  pl.BlockSpec((1, D), lambda b, i: (0, 0)),
                  pl.BlockSpec(wq.shape, lambda b, i: (0, 0)),
                  pl.BlockSpec(keys.shape, lambda b, i: (0, 0, 0))],
        out_specs=[pl.BlockSpec((D, tm), lambda b, i: (0, b * nt + i)), hk_spec, hk_spec, hk_spec,
                   pl.BlockSpec((PEER_HEADS, SUBLANES, tm), lambda b, i: (0, 0, b * nt + i))],
        out_shape=[jax.ShapeDtypeStruct((D, N), BF16), hkn, hkn, hkn,
                   jax.ShapeDtypeStruct((PEER_HEADS, SUBLANES, N), F32)],
        compiler_params=_params("parallel", "parallel"),
        name="peer_route",
    )(x, scale, shift, g.reshape(1, D), wq, keys)


def _peer_expert_kernel(ht_ref, s1_ref, s2_ref, e2_ref, st_ref, u_ref, vt_ref, x_ref, gt_ref, o_ref, acc_ref):
    e = pl.program_id(2)

    @pl.when(e == 0)
    def _():
        acc_ref[...] = jnp.zeros(acc_ref.shape, F32)

    ht = ht_ref[...]
    upd = None
    group = 2 * PEER_KEYS
    for gi in range(u_ref.shape[0] // group):
        act = jnp.dot(u_ref[gi * group:(gi + 1) * group, :], ht, preferred_element_type=F32)
        ws = []
        for r in range(2 * gi, 2 * gi + 2):
            gate = None
            for hd in range(PEER_HEADS):
                s1 = s1_ref[hd, r:r + 1, :]
                e1 = jnp.exp(s1 + st_ref[hd, 1:2, :]) * st_ref[hd, 2:3, :]
                gh = jnp.where(s2_ref[hd] + s1 >= st_ref[hd, 0:1, :], e2_ref[hd] * e1, 0.0)
                gate = gh if gate is None else gate + gh
            a = act[(r - 2 * gi) * PEER_KEYS:(r - 2 * gi + 1) * PEER_KEYS]
            gelu = 0.5 * a * (1.0 + lax.erf(a * (2.0 ** -0.5)))
            ws.append((gate * gelu).astype(BF16))
        d = jnp.dot(vt_ref[:, gi * group:(gi + 1) * group], jnp.concatenate(ws, axis=0),
                    preferred_element_type=F32)
        upd = d if upd is None else upd + d
    acc_ref[...] += upd

    @pl.when(e == pl.num_programs(2) - 1)
    def _():
        o_ref[0] = x_ref[0] + gt_ref[0] * acc_ref[...].T


def peer_expert(ht, s1, s2, e2, st, u, vt, x, gt):
    B, T, D = x.shape
    E = u.shape[0]
    tm = _row_tile(T, 256)
    nt = T // tm
    te = SUBLANES * PEER_KEYS
    tmod = tm if gt.shape[1] == T else 1
    mod_map = (lambda b, i, e: (b, i, 0)) if gt.shape[1] == T else (lambda b, i, e: (b, 0, 0))
    hk_spec = pl.BlockSpec((PEER_HEADS, PEER_KEYS, tm), lambda b, i, e: (0, 0, b * nt + i))
    return pl.pallas_call(
        _peer_expert_kernel,
        grid=(B, nt, E // te),
        in_specs=[pl.BlockSpec((D, tm), lambda b, i, e: (0, b * nt + i)),
                  pl.BlockSpec((PEER_HEADS, SUBLANES, tm), lambda b, i, e: (0, e, b * nt + i)),
                  hk_spec, hk_spec,
                  pl.BlockSpec((PEER_HEADS, SUBLANES, tm), lambda b, i, e: (0, 0, b * nt + i)),
                  pl.BlockSpec((te, D), lambda b, i, e: (e, 0)),
                  pl.BlockSpec((D, te), lambda b, i, e: (0, e)),
                  pl.BlockSpec((1, tm, D), lambda b, i, e: (b, i, 0)),
                  pl.BlockSpec((1, tmod, D), mod_map)],
        out_specs=pl.BlockSpec((1, tm, D), lambda b, i, e: (b, i, 0)),
        out_shape=jax.ShapeDtypeStruct((B, T, D), F32),
        scratch_shapes=[pltpu.VMEM((D, tm), F32)],
        compiler_params=_params("parallel", "parallel", "arbitrary"),
        name="peer_expert",
    )(ht, s1, s2, e2, st, u, vt, x, gt)


def _pack_state(s):
    B, H, dv, dk = s.shape
    return s.reshape(B, H // 2, 2, dv, dk).transpose(0, 1, 3, 2, 4).reshape(B * H // 2, dv, 2 * dk)


def _unpack_state(s, B):
    P, dv, dk2 = s.shape
    H = 2 * P // B
    return s.reshape(B, H // 2, dv, 2, dk2 // 2).transpose(0, 1, 3, 2, 4).reshape(B, H, dv, dk2 // 2)


def _pack_state_bd(s):
    B, H, dv, dk = s.shape
    st = s.transpose(0, 1, 3, 2).reshape(B, H // 2, 2, dk, dv)
    z = jnp.zeros((B, H // 2, dk, dv), s.dtype)
    top = jnp.concatenate([st[:, :, 0], z], axis=-1)
    bot = jnp.concatenate([z, st[:, :, 1]], axis=-1)
    return jnp.concatenate([top, bot], axis=-2).reshape(B * H // 2, 2 * dk, 2 * dv)


def _unpack_state_bd(s, B):
    P, dk2, dv2 = s.shape
    s = s.reshape(B, P // B, 2, dk2 // 2, 2, dv2 // 2)
    d = jnp.stack([s[:, :, 0, :, 0, :], s[:, :, 1, :, 1, :]], axis=2)
    return d.reshape(B, 2 * P // B, dk2 // 2, dv2 // 2).transpose(0, 1, 3, 2)


def _block(x, ada, pos0, past, shift_prev, state, lw, rowwise_mods):
    B, T, D = x.shape
    N = B * T
    sh1, sc1, gt1, sh2, sc2, gt2 = ada
    if rowwise_mods:
        xr = x.reshape(1, N, D)
        mod = lambda m: jnp.repeat(m, T, axis=0).reshape(1, N, D)
    else:
        xr = x
        mod = lambda m: m.reshape(B, 1, D)
    h = modulate(xr, mod(sc1), mod(sh1), lw["norm_mix"], BF16).reshape(N, D)
    q3 = matmul(h, lw["w_q3"])
    mrow = matmul(h, lw["w_moba"])
    zr = matmul(h, lw["w_rw"], tn_want=lw["w_rw"].shape[1])
    drow = matmul(h, lw["w_dsa"])
    gl = matmul(h, lw["w_gl"], out_dtype=BF16)
    MR = mrow.shape[1]
    RW = zr.shape[1]
    q3b = q3.reshape(B, T, -1)
    drow_b = drow.reshape(B, T, -1)

    if past is None:
        ident = jnp.arange(N // PAGE_SIZE, dtype=jnp.int32).reshape(B, T // PAGE_SIZE)
        kv_a, km = paged_cast(mrow.reshape(1, N // PAGE_SIZE, PAGE_SIZE, MR), 0, ident, None, MR // 2)
        kv_c = paged_cast(drow.reshape(1, N // PAGE_SIZE, PAGE_SIZE, drow.shape[1]), 0, ident, None)
    else:
        pages_a, pages_c, layer, table = past
        pad = lambda r: jnp.pad(r, ((0, 0), (0, MOBA_BLOCK - T), (0, 0)))
        kv_a, km = paged_cast(pages_a, layer, table, pad(mrow.reshape(B, T, MR)), MR // 2)
        kv_c = paged_cast(pages_c, layer, table, pad(drow_b[:, :, :DSA_ROW]))
    nbp = -(-km.shape[1] // 16) * 16
    km = jnp.pad(km, ((0, 0), (0, nbp - km.shape[1]), (0, 0)))
    ya = moba_attention(q3b, kv_a, km, pos0)
    yc = dsa_attention(q3b, drow_b, kv_c, pos0, pos0 + T)

    zr_b = zr.reshape(B, T, RW)
    zs = jnp.concatenate([shift_prev[:, None], zr_b[:, :-1]], axis=1)
    prep_w = (lw["rwkv_mu"], lw["rwkv_lora"], lw["rwkv_vecs"], lw["ones_blk"])
    if T % (2 * RWKV_CHUNK) == 0:
        rs, gb, t3 = rwkv_prep(zr_b, zs, *prep_w, True)
        yb, s_new = rwkv_chunk_scan(rs, t3, gb, _pack_state_bd(state), lw["rwkv_ln_w"], lw["rwkv_ln_b"])
        s_new = _unpack_state_bd(s_new, B)
    else:
        rs, gb = rwkv_prep(zr_b.reshape(1, N, RW), zs.reshape(1, N, RW), *prep_w, False)
        Tp = -(-T // LANES) * LANES
        padt = lambda a: jnp.pad(a.reshape(B, T, -1), ((0, 0), (0, Tp - T), (0, 0)))
        A = gb.shape[2] // 2
        rs_p = padt(rs).at[:, T:, A:2 * A].set(1.0)
        yb, s_new = rwkv_scan(rs_p, padt(gb), _pack_state(state), lw["rwkv_ln_w"], lw["rwkv_ln_b"], T)
        yb = yb[:, :T]
        s_new = _unpack_state(s_new, B)

    W = ya.shape[2]
    rsh = (lambda a: a.reshape(1, N, -1)) if rowwise_mods else (lambda a: a.reshape(B, T, -1))
    x1 = merge(rsh(ya), rsh(yb), rsh(yc), rsh(gl), xr, mod(gt1), lw["w_branch"], lw["w_out"])
    ht, s1, s2, e2, st = peer_route(x1, mod(sc2), mod(sh2), lw["norm_ffn"], lw["peer_wq"], lw["peer_keys"])
    x2 = peer_expert(ht, s1, s2, e2, st, lw["peer_u"], lw["peer_vt"], x1, mod(gt2))
    return (x2.reshape(B, T, D), mrow.reshape(B, T, MR), drow_b[:, :, :DSA_ROW],
            s_new, zr_b[:, -1])


def kernel(x_prompt, x_sample, cache_moba, cache_dsa, state_rwkv, state_shift, page_table, c_prompt, c_sample, w_ada, b_ada, norm_mix, w_in, rwkv_mu, rwkv_w0, rwkv_w_up, rwkv_a0, rwkv_a_up, rwkv_k_k, rwkv_k_a, rwkv_r_k, rwkv_ln_w, rwkv_ln_b, w_branch, w_out, norm_ffn, peer_wq, peer_keys, peer_u, peer_v, norm_final):
    n_b, n_t, D = x_prompt.shape
    n_db, n_dt, _ = x_sample.shape
    depth = w_in.shape[0]
    past_len = page_table.shape[1] * PAGE_SIZE
    A = D // 2
    RW = 4 * A + 2 * B_LORA
    idx_w = IDX_HEADS * IDX_DIM
    sizes = (A, 2 * A, RW, A, idx_w, IDX_HEADS, DSA_ROW, N_BRANCH * D)
    off = np.concatenate([[0], np.cumsum(sizes)])
    assert off[-1] == w_in.shape[2]

    n_c = n_b + n_db
    n_cp = -(-n_c // SUBLANES) * SUBLANES
    c_all = jnp.pad(jnp.concatenate([c_prompt, c_sample], axis=0), ((0, n_cp - n_c), (0, 0)))
    ada_all = ada_matmul(c_all, w_ada, b_ada)

    ones_blk = jnp.asarray(np.kron(np.eye(A // D_HEAD), np.ones((D_HEAD, D_HEAD))), BF16)
    xp, xs = x_prompt, x_sample
    outs = [[] for _ in range(8)]
    for l in range(depth):
        wl = w_in[l]
        seg = lambda n: wl[:, off[n]:off[n + 1]]
        dsa_pad = LANES * (-(-(DSA_ROW + IDX_HEADS) // LANES)) - DSA_ROW - IDX_HEADS
        lora = jnp.zeros((2 * B_LORA, 2 * A), F32)
        lora = lora.at[:B_LORA, :A].set(rwkv_w_up[l]).at[B_LORA:, A:].set(rwkv_a_up[l])
        flat = lambda a: a.reshape(1, -1)
        lw = dict(
            norm_mix=norm_mix[l], norm_ffn=norm_ffn[l],
            w_q3=jnp.concatenate([seg(0), seg(3), seg(4)], axis=1).astype(BF16),
            w_moba=seg(1).astype(BF16), w_rw=seg(2).astype(BF16),
            w_dsa=jnp.pad(jnp.concatenate([seg(6), seg(5)], axis=1), ((0, 0), (0, dsa_pad))).astype(BF16),
            w_gl=seg(7).astype(BF16),
            rwkv_mu=flat(rwkv_mu[l]), rwkv_lora=lora, ones_blk=ones_blk,
            rwkv_vecs=jnp.concatenate([flat(rwkv_w0[l]), flat(rwkv_a0[l]), flat(rwkv_k_k[l]), flat(rwkv_k_a[l]),
                                       flat(rwkv_r_k[l]), jnp.zeros((SUBLANES - 5, A), F32)], axis=0),
            rwkv_ln_w=rwkv_ln_w[l], rwkv_ln_b=rwkv_ln_b[l],
            w_branch=w_branch[l].astype(BF16), w_out=w_out[l].astype(BF16),
            peer_wq=peer_wq[l].astype(BF16),
            peer_keys=peer_keys[l].reshape(2 * PEER_HEADS, PEER_KEYS, -1).astype(BF16),
            peer_u=peer_u[l].astype(BF16), peer_vt=peer_v[l].T.astype(BF16),
        )
        ada_p = jnp.split(ada_all[l, :n_b], 6, axis=-1)
        ada_s = jnp.split(ada_all[l, n_b:n_c], 6, axis=-1)
        xp, m_p, d_p, r_p, s_p = _block(
            xp, ada_p, 0, None, jnp.zeros((n_b, RW), F32),
            jnp.zeros((n_b, A // D_HEAD, D_HEAD, D_HEAD), F32), lw, False)
        xs, m_s, d_s, r_s, s_s = _block(
            xs, ada_s, past_len, (cache_moba, cache_dsa, l, page_table), state_shift[l], state_rwkv[l], lw, True)
        for lst, val in zip(outs, (m_p, m_s, d_p, d_s, r_p, r_s, s_p, s_s)):
            lst.append(val)
    y_prompt = rms_norm_rows(xp.reshape(n_b * n_t, D), norm_final).reshape(n_b, n_t, D)
    y_sample = rms_norm_rows(xs.reshape(n_db * n_dt, D), norm_final).reshape(n_db, n_dt, D)
    return (y_prompt, y_sample) + tuple(jnp.stack(o) for o in outs)
```

```python
import functools

import jax
import jax.numpy as jnp
import numpy as np
from jax import lax
from jax.experimental import pallas as pl
from jax.experimental.pallas import tpu as pltpu

D_HEAD = 64
MOBA_BLOCK = 256
MOBA_TOPK = 3
B_LORA = 64
C_KV_HEADS = 2
IDX_HEADS = 8
IDX_DIM = 64
DSA_TOPK = 256
N_BRANCH = 3
PEER_KEYS = 128
PEER_HEADS = 8
PEER_TOPK = 16
PAGE_SIZE = 128
EPS = 1e-6
RWKV_GN_EPS = 64e-5
NEG = -1e30

LANES = 128
SUBLANES = 8
VMEM_LIMIT_BYTES = 56 * 1024 * 1024

F32 = jnp.float32
BF16 = jnp.bfloat16
INT_MIN = -2 ** 31


def _params(*sem):
    return pltpu.CompilerParams(dimension_semantics=sem, vmem_limit_bytes=VMEM_LIMIT_BYTES)


def _split_bf16(a):
    hi = a.astype(BF16)
    lo = (a - hi.astype(F32)).astype(BF16)
    return hi, lo


def _dot3(a, b, dims):
    ah, al = _split_bf16(a)
    bh, bl = _split_bf16(b)
    dn = (dims, ((), ()))
    d = lambda x, y: lax.dot_general(x, y, dn, preferred_element_type=F32)
    return d(ah, bh) + (d(ah, bl) + d(al, bh))


_NN = ((1,), (0,))
_NT = ((1,), (1,))


def _dot_bf16(a, b, dims=_NN):
    return lax.dot_general(a.astype(BF16), b.astype(BF16), (dims, ((), ())), preferred_element_type=F32)


def _row_tile(n, want):
    t = min(n, want)
    while n % t:
        t //= 2
    return t


def _modulate_kernel(x_ref, sc_ref, sh_ref, g_ref, o_ref):
    x = x_ref[0]
    y = x * lax.rsqrt(jnp.mean(x * x, axis=-1, keepdims=True) + EPS) * g_ref[...]
    o_ref[0] = (y * (1.0 + sc_ref[0]) + sh_ref[0]).astype(o_ref.dtype)


def modulate(x, scale, shift, g, out_dtype):
    B, T, D = x.shape
    tm = _row_tile(T, 512)
    tmod = tm if scale.shape[1] == T else 1
    mod_map = (lambda b, i: (b, i, 0)) if scale.shape[1] == T else (lambda b, i: (b, 0, 0))
    return pl.pallas_call(
        _modulate_kernel,
        grid=(B, T // tm),
        in_specs=[pl.BlockSpec((1, tm, D), lambda b, i: (b, i, 0)),
                  pl.BlockSpec((1, tmod, D), mod_map),
                  pl.BlockSpec((1, tmod, D), mod_map),
                  pl.BlockSpec((1, D), lambda b, i: (0, 0))],
        out_specs=pl.BlockSpec((1, tm, D), lambda b, i: (b, i, 0)),
        out_shape=jax.ShapeDtypeStruct((B, T, D), out_dtype),
        compiler_params=_params("parallel", "parallel"),
        name="modulate",
    )(x, scale, shift, g.reshape(1, D))


def _rmsnorm_kernel(x_ref, g_ref, o_ref):
    x = x_ref[...]
    o_ref[...] = x * lax.rsqrt(jnp.mean(x * x, axis=-1, keepdims=True) + EPS) * g_ref[...]


def _transpose_cast_kernel(x_ref, o_ref):
    o_ref[...] = x_ref[...].T.astype(o_ref.dtype)


def transpose_cast(x, dtype):
    R, C = x.shape
    tr = _row_tile(R, 1024)
    return pl.pallas_call(
        _transpose_cast_kernel,
        grid=(R // tr,),
        in_specs=[pl.BlockSpec((tr, C), lambda i: (i, 0))],
        out_specs=pl.BlockSpec((C, tr), lambda i: (0, i)),
        out_shape=jax.ShapeDtypeStruct((C, R), dtype),
        compiler_params=_params("parallel"),
        name="transpose_cast",
    )(x)


def rms_norm_rows(x, g):
    N, D = x.shape
    tm = _row_tile(N, 512)
    return pl.pallas_call(
        _rmsnorm_kernel,
        grid=(N // tm,),
        in_specs=[pl.BlockSpec((tm, D), lambda i: (i, 0)), pl.BlockSpec((1, D), lambda i: (0, 0))],
        out_specs=pl.BlockSpec((tm, D), lambda i: (i, 0)),
        out_shape=jax.ShapeDtypeStruct((N, D), F32),
        compiler_params=_params("parallel"),
        name="final_norm",
    )(x, g.reshape(1, D))


def _mm_kernel(a_ref, w_ref, o_ref):
    o_ref[...] = jnp.dot(a_ref[...], w_ref[...], preferred_element_type=F32).astype(o_ref.dtype)


def _mm3_kernel(a_ref, w_ref, b_ref, o_ref):
    o_ref[0] = _dot3(a_ref[...], w_ref[0], _NN) + b_ref[0]


def matmul(a, w, out_dtype=F32, tn_want=1024):
    N, K = a.shape
    C = w.shape[1]
    tm = _row_tile(N, 512)
    tn = _row_tile(C, tn_want)
    return pl.pallas_call(
        _mm_kernel,
        grid=(C // tn, N // tm),
        in_specs=[pl.BlockSpec((tm, K), lambda j, i: (i, 0)), pl.BlockSpec((K, tn), lambda j, i: (0, j))],
        out_specs=pl.BlockSpec((tm, tn), lambda j, i: (i, j)),
        out_shape=jax.ShapeDtypeStruct((N, C), out_dtype),
        compiler_params=_params("parallel", "parallel"),
        name="matmul",
    )(a, w)


def ada_matmul(c, w, b):
    M, D = c.shape
    L, _, C = w.shape
    tn = _row_tile(C, 1536)
    return pl.pallas_call(
        _mm3_kernel,
        grid=(L, C // tn),
        in_specs=[pl.BlockSpec((M, D), lambda l, j: (0, 0)),
                  pl.BlockSpec((1, D, tn), lambda l, j: (l, 0, j)),
                  pl.BlockSpec((1, 1, tn), lambda l, j: (l, 0, j))],
        out_specs=pl.BlockSpec((1, M, tn), lambda l, j: (l, 0, j)),
        out_shape=jax.ShapeDtypeStruct((L, M, C), F32),
        compiler_params=_params("parallel", "parallel"),
        name="ada_matmul",
    )(c, w, b.reshape(L, 1, C))


def _paged_cast_kernel(tbl_ref, p0_ref, p1_ref, tail_ref, o_ref, *mean_ref, n_steps, mean_cols):
    j = pl.program_id(1)

    def emit(rows):
        o_ref[0] = rows.astype(o_ref.dtype)
        if mean_ref:
            mean_ref[0][0] = jnp.mean(rows[:, :mean_cols], axis=0, keepdims=True)

    @pl.when(j < n_steps)
    def _():
        emit(jnp.concatenate([p0_ref[0, 0], p1_ref[0, 0]], axis=0))

    @pl.when(j >= n_steps)
    def _():
        emit(tail_ref[0])


def paged_cast(pages, layer, table, tail, mean_cols=0):
    _, _, page, R = pages.shape
    B, n_pg = table.shape
    assert page == PAGE_SIZE and MOBA_BLOCK == 2 * PAGE_SIZE and n_pg % 2 == 0
    n_steps = n_pg // 2
    has_tail = tail is not None
    n_blk = n_steps + (1 if has_tail else 0)
    if not has_tail:
        tail = jnp.zeros((B, MOBA_BLOCK, R), F32)
    last = n_steps - 1

    def page_map(k):
        return lambda b, j, tbl: (layer, tbl[b, 2 * jnp.minimum(j, last) + k], 0, 0)

    out_shape = [jax.ShapeDtypeStruct((B, n_blk * MOBA_BLOCK, R), BF16)]
    out_specs = [pl.BlockSpec((1, MOBA_BLOCK, R), lambda b, j, tbl: (b, j, 0))]
    if mean_cols:
        out_shape.append(jax.ShapeDtypeStruct((B * n_blk, 1, mean_cols), F32))
        out_specs.append(pl.BlockSpec((1, 1, mean_cols), lambda b, j, tbl: (b * n_blk + j, 0, 0)))
    res = pl.pallas_call(
        functools.partial(_paged_cast_kernel, n_steps=n_steps, mean_cols=mean_cols),
        grid_spec=pltpu.PrefetchScalarGridSpec(
            num_scalar_prefetch=1,
            grid=(B, n_blk),
            in_specs=[pl.BlockSpec((1, 1, PAGE_SIZE, R), page_map(0)),
                      pl.BlockSpec((1, 1, PAGE_SIZE, R), page_map(1)),
                      pl.BlockSpec((1, MOBA_BLOCK, R), lambda b, j, tbl: (b, 0, 0))],
            out_specs=out_specs),
        out_shape=out_shape,
        compiler_params=_params("parallel", "arbitrary"),
        name="paged_cast",
    )(table, pages, pages, tail)
    if mean_cols:
        return res[0], res[1].reshape(B, n_blk, mean_cols)
    return res[0]


CAUSAL_CLASSES = 8


def _prefix_lengths(L):
    n = CAUSAL_CLASSES
    while L % (n * MOBA_BLOCK):
        n //= 2
    return [L * (c + 1) // n for c in range(n)]


def _for_visible_prefix(L, last_pos, body):
    prev = 0
    for lc in _prefix_lengths(L):
        pl.when((last_pos >= prev) & (last_pos < lc))(functools.partial(body, lc))
        prev = lc


def _moba_kernel(q_ref, k_ref, v_ref, km_ref, o_ref, *, pos0, tq):
    i = pl.program_id(2)
    body = functools.partial(_moba_body, q_ref, k_ref, v_ref, km_ref, o_ref, pos0 + i * tq, tq)
    _for_visible_prefix(k_ref.shape[1], pos0 + (i + 1) * tq - 1, body)


def _moba_body(q_ref, k_ref, v_ref, km_ref, o_ref, q_pos0, tq, L):
    nbp = km_ref.shape[1]
    q = q_ref[0] * (D_HEAD ** -0.5)
    k = k_ref[0, :L]
    v = v_ref[0, :L]
    km = km_ref[0]
    lane = lax.broadcasted_iota(jnp.int32, (1, LANES), 1)
    q_pos = q_pos0 + lax.broadcasted_iota(jnp.int32, (tq, 1), 0)
    own = q_pos // MOBA_BLOCK
    key_pos = lax.broadcasted_iota(jnp.int32, (1, L), 1)
    causal = jnp.where(key_pos > q_pos, NEG, 0.0)
    blk_id = lax.broadcasted_iota(jnp.int32, (tq, nbp), 1).astype(F32)
    own = own.astype(F32)
    expand =jnp.where(lax.broadcasted_iota(jnp.int32, (nbp, L), 1) // MOBA_BLOCK
                       == lax.broadcasted_iota(jnp.int32, (nbp, L), 0), 1.0, 0.0).astype(BF16)
    past = blk_id < own
    outs = []
    for h in range(2):
        qh = jnp.where(lane // D_HEAD == h, q, 0.0)
        g = jnp.where(past, _dot3(qh, km, _NT), NEG)
        closed = jnp.where(blk_id == own, 0.0, NEG)
        for _ in range(MOBA_TOPK):
            m = jnp.max(g, axis=-1, keepdims=True)
            first = jnp.min(jnp.where(g == m, blk_id, nbp), axis=-1, keepdims=True)
            pick = blk_id == first
            closed = jnp.where(pick & past, 0.0, closed)
            g = jnp.where(pick, -jnp.inf, g)
        s = _dot_bf16(qh, k, _NT) + _dot_bf16(closed, expand) + causal
        p = jnp.exp(s - jnp.max(s, axis=-1, keepdims=True))
        o = _dot_bf16(p, v) / jnp.sum(p, axis=-1, keepdims=True)
        outs.append(o)
    o_ref[0] = jnp.where(lane // D_HEAD == 0, outs[0], outs[1])


def moba_attention(q3, kv, km, pos0):
    B, T, _ = q3.shape
    L = kv.shape[1]
    nbp = km.shape[1]
    tq = _row_tile(T, 128)
    n_pair = kv.shape[2] // (2 * LANES)
    return pl.pallas_call(
        functools.partial(_moba_kernel, pos0=pos0, tq=tq),
        grid=(B, n_pair, T // tq),
        in_specs=[pl.BlockSpec((1, tq, LANES), lambda b, hp, i: (b, i, hp)),
                  pl.BlockSpec((1, L, LANES), lambda b, hp, i: (b, 0, hp)),
                  pl.BlockSpec((1, L, LANES), lambda b, hp, i: (b, 0, n_pair + hp)),
                  pl.BlockSpec((1, nbp, LANES), lambda b, hp, i: (b, 0, hp))],
        out_specs=pl.BlockSpec((1, tq, LANES), lambda b, hp, i: (b, i, hp)),
        out_shape=jax.ShapeDtypeStruct((B, T, n_pair * LANES), F32),
        compiler_params=_params("parallel", "parallel", "parallel"),
        name="moba_attention",
    )(q3, kv, kv, km)


DSA_ROW = 2 * C_KV_HEADS * D_HEAD + IDX_DIM


def _dsa_kernel(qc_ref, qi_ref, wi_ref, kv_ref, o_ref, key_full, bias_full, *, pos0, tq, topk, n_heads):
    i = pl.program_id(1)
    body = functools.partial(_dsa_body, qc_ref, qi_ref, wi_ref, kv_ref, o_ref, key_full, bias_full,
                             pos0 + i * tq, tq, topk, n_heads)
    _for_visible_prefix(kv_ref.shape[1], pos0 + (i + 1) * tq - 1, body)


def _dsa_body(qc_ref, qi_ref, wi_ref, kv_ref, o_ref, key_full, bias_full, q_pos0, tq, topk, n_heads, L):
    key_scr = key_full.at[:, :L]
    bias_scr = bias_full.at[:, :L]
    kvw = C_KV_HEADS * D_HEAD
    k2 = kv_ref[0, :L, 0:kvw]
    v2 = kv_ref[0, :L, kvw:2 * kvw]
    ki = kv_ref[0, :L, 2 * kvw:2 * kvw + IDX_DIM]
    qc = qc_ref[0] * (D_HEAD ** -0.5)
    qi = qi_ref[0]
    wi = wi_ref[0][:, DSA_ROW:DSA_ROW + IDX_HEADS] * (IDX_HEADS ** -0.5)
    q_pos = q_pos0 + lax.broadcasted_iota(jnp.int32, (tq, 1), 0)
    key_pos = lax.broadcasted_iota(jnp.int32, (1, L), 1)
    valid = key_pos <= q_pos

    score = jnp.zeros((tq, L), F32)
    for h in range(IDX_HEADS):
        lg = _dot_bf16(qi[:, h * IDX_DIM:(h + 1) * IDX_DIM], ki, _NT)
        score = score + jnp.maximum(lg, 0.0) * wi[:, h:h + 1]
    _dsa_select(score, valid, q_pos, None, key_scr, bias_scr, topk)

    bias = bias_scr[...]
    grp = n_heads // C_KV_HEADS
    zero = jnp.zeros((tq, D_HEAD), F32)
    outs = []
    for h in range(n_heads):
        g = h // grp
        qh = qc[:, h * D_HEAD:(h + 1) * D_HEAD]
        q128 = jnp.concatenate([qh, zero] if g == 0 else [zero, qh], axis=1)
        s = _dot_bf16(q128, k2, _NT) + bias
        p = jnp.exp(s - jnp.max(s, axis=-1, keepdims=True))
        o = _dot_bf16(p, v2) / jnp.sum(p, axis=-1, keepdims=True)
        outs.append(o[:, g * D_HEAD:(g + 1) * D_HEAD])
    o_ref[0] = jnp.concatenate(outs, axis=1)


def _dsa_select(score, valid, q_pos, row_ok, key_scr, bias_scr, topk):
    tq, L = score.shape
    score = jnp.where(valid, score, NEG)
    score = jnp.where(score == 0.0, 0.0, score)
    bits = pltpu.bitcast(score, jnp.int32)
    key = jnp.where(bits < 0, bits ^ 0x7FFFFFFF, bits)
    key_scr[...] = key

    def count_ge(c):
        return jnp.sum(jnp.where(key_scr[...] >= c, 1.0, 0.0), axis=-1, keepdims=True)

    t0 = jnp.where(count_ge(jnp.zeros((tq, 1), jnp.int32)) >= topk, 0, INT_MIN).astype(jnp.int32)

    def bit_step(it, t):
        c = t | lax.shift_left(jnp.int32(1), 30 - it)
        return jnp.where(count_ge(c) >= topk, c, t)

    thr = lax.fori_loop(0, 31, bit_step, t0)
    key = key_scr[...]
    take = (key >= thr) & valid
    bias_scr[...] = jnp.where(take, 0.0, NEG)
    n_take = jnp.sum(jnp.where(take, 1.0, 0.0), axis=-1, keepdims=True)
    if row_ok is not None:
        n_take = jnp.where(row_ok, n_take, 0.0)

    @pl.when(jnp.max(n_take) > topk)
    def _():
        above = (key > thr) & valid
        need = topk - jnp.sum(jnp.where(above, 1.0, 0.0), axis=-1, keepdims=True)
        upper = jnp.where(lax.broadcasted_iota(jnp.int32, (LANES, LANES), 0)
                          < lax.broadcasted_iota(jnp.int32, (LANES, LANES), 1), 1.0, 0.0).astype(BF16)

        def blk(j, seen):
            sl = pl.ds(pl.multiple_of(j * LANES, LANES), LANES)
            kj = key_scr[:, sl]
            ok = (j * LANES + lax.broadcasted_iota(jnp.int32, (1, LANES), 1)) <= q_pos
            tied = jnp.where((kj == thr) & ok, 1.0, 0.0)
            before = seen + _dot_bf16(tied, upper)
            keep = ((kj > thr) & ok) | ((tied > 0.0) & (before < need))
            bias_scr[:, sl] = jnp.where(keep, 0.0, NEG)
            return seen + jnp.sum(tied, axis=-1, keepdims=True)

        lax.fori_loop(0, L // LANES, blk, jnp.zeros((tq, 1), F32))


def _moba_decode_kernel(tbl_ref, p0_ref, p1_ref, tail_ref, q_ref, o_ref, o_scr, m_scr, l_scr, g_scr,
                        *, n_steps, t_new, n_heads):
    j = pl.program_id(1)
    W = n_heads * D_HEAD
    R = q_ref.shape[1]
    lane = lax.broadcasted_iota(jnp.int32, (1, LANES), 1)
    head_cols = (lax.broadcasted_iota(jnp.int32, (R, W), 1) // D_HEAD
                 == lax.broadcasted_iota(jnp.int32, (R, W), 0) % n_heads)
    qm = jnp.where(head_cols, q_ref[0], 0.0)
    qs = qm * (D_HEAD ** -0.5)

    @pl.when(j == 0)
    def _():
        m_scr[...] = jnp.zeros(m_scr.shape, F32)
        l_scr[...] = jnp.zeros(l_scr.shape, F32)
        g_scr[...] = jnp.full(g_scr.shape, NEG, F32)

    @pl.when(j < n_steps)
    def _():
        rows = jnp.concatenate([p0_ref[0, 0], p1_ref[0, 0]], axis=0)
        k, v = rows[:, :W], rows[:, W:]
        gate = jnp.sum(qm * jnp.mean(k, axis=0, keepdims=True), axis=-1, keepdims=True)
        s = _dot_bf16(qs, k, _NT)
        m = jnp.max(s, axis=-1, keepdims=True)
        p = jnp.exp(s - m)
        o_scr[j] = _dot_bf16(p, v)
        here = lane == j
        m_scr[...] = jnp.where(here, m, m_scr[...])
        l_scr[...] = jnp.where(here, jnp.sum(p, axis=-1, keepdims=True), l_scr[...])
        g_scr[...] = jnp.where(here, gate, g_scr[...])

    @pl.when(j == n_steps)
    def _():
        tail = tail_ref[0]
        s = _dot_bf16(qs, tail[:, :W], _NT)
        n_t = tail.shape[0]
        seen = (lax.broadcasted_iota(jnp.int32, (R, n_t), 1)
                <= lax.broadcasted_iota(jnp.int32, (R, n_t), 0) // n_heads)
        s = jnp.where(seen, s, NEG)
        m_own = jnp.max(s, axis=-1, keepdims=True)
        p = jnp.exp(s - m_own)
        l_own = jnp.sum(p, axis=-1, keepdims=True)
        o_own = _dot_bf16(p, tail[:, W:])
        blk = lane.astype(F32)
        g = g_scr[...]
        sel = jnp.zeros(g.shape, F32)
        for _ in range(MOBA_TOPK):
            mx = jnp.max(g, axis=-1, keepdims=True)
            first = jnp.min(jnp.where(g == mx, blk, float(LANES)), axis=-1, keepdims=True)
            pick = blk == first
            sel = jnp.where(pick & (lane < n_steps), 1.0, sel)
            g = jnp.where(pick, -jnp.inf, g)
        m_all = jnp.maximum(m_own, jnp.max(jnp.where(sel > 0.0, m_scr[...], -jnp.inf), axis=-1, keepdims=True))
        wgt = jnp.where(sel > 0.0, jnp.exp(m_scr[...] - m_all), 0.0)
        w_own = jnp.exp(m_own - m_all)
        denom = jnp.sum(wgt * l_scr[...], axis=-1, keepdims=True) + w_own * l_own

        def add_block(jj, acc):
            return acc + jnp.sum(jnp.where(lane == jj, wgt, 0.0), axis=-1, keepdims=True) * o_scr[jj]

        acc = lax.fori_loop(0, n_steps, add_block, w_own * o_own)
        res = jnp.where(head_cols, acc / denom, 0.0)
        o_ref[0] = jnp.sum(res.reshape(R // n_heads, n_heads, W), axis=1)


def moba_decode(q, pages, layer, table, new_rows):
    B, T, W = q.shape
    n_heads = W // D_HEAD
    n_pg = table.shape[1]
    assert n_pg % 2 == 0 and n_pg // 2 <= LANES and T <= 16
    n_steps = n_pg // 2
    last = n_steps - 1
    R2 = pages.shape[3]
    q_rows = jnp.repeat(q, n_heads, axis=1)
    tail = jnp.pad(new_rows, ((0, 0), (0, 16 - T), (0, 0)))

    def page_map(k):
        return lambda b, j, tbl: (layer, tbl[b, 2 * jnp.minimum(j, last) + k], 0, 0)

    R = T * n_heads
    return pl.pallas_call(
        functools.partial(_moba_decode_kernel, n_steps=n_steps, t_new=T, n_heads=n_heads),
        grid_spec=pltpu.PrefetchScalarGridSpec(
            num_scalar_prefetch=1,
            grid=(B, n_steps + 1),
            in_specs=[pl.BlockSpec((1, 1, PAGE_SIZE, R2), page_map(0)),
                      pl.BlockSpec((1, 1, PAGE_SIZE, R2), page_map(1)),
                      pl.BlockSpec((1, 16, R2), lambda b, j, tbl: (b, 0, 0)),
                      pl.BlockSpec((1, R, W), lambda b, j, tbl: (b, 0, 0))],
            out_specs=pl.BlockSpec((1, T, W), lambda b, j, tbl: (b, 0, 0)),
            scratch_shapes=[pltpu.VMEM((n_steps, R, W), F32), pltpu.VMEM((R, LANES), F32),
                            pltpu.VMEM((R, LANES), F32), pltpu.VMEM((R, LANES), F32)]),
        out_shape=jax.ShapeDtypeStruct((B, T, W), F32),
        compiler_params=_params("parallel", "arbitrary"),
        name="moba_decode",
    )(table, pages, pages, tail, q_rows)


def _dsa_decode_kernel(tbl_ref, p0_ref, p1_ref, tail_ref, qi_ref, wi_ref, qc_ref, o_ref,
                       kv_scr, sc_scr, key_scr, bias_scr, *, n_steps, t_new, topk, n_heads, pos0):
    j = pl.program_id(1)
    kvw = C_KV_HEADS * D_HEAD
    TQ = SUBLANES
    wide = lambda a, n: jnp.concatenate([a] * (n // LANES), axis=1)

    def scores(ki):
        lg = _dot_bf16(qi_ref[0], ki, _NT)
        sc = jnp.maximum(lg, 0.0) * wide(wi_ref[0], lg.shape[1])
        return jnp.sum(sc.reshape(TQ, IDX_HEADS, lg.shape[1]), axis=1)

    @pl.when(j < n_steps)
    def _():
        rows = jnp.concatenate([p0_ref[0, 0], p1_ref[0, 0]], axis=0)
        at = pl.ds(pl.multiple_of(j * MOBA_BLOCK, MOBA_BLOCK), MOBA_BLOCK)
        kv_scr[at, :] = rows[:, :2 * kvw].astype(BF16)
        sc_scr[:, at] = scores(rows[:, 2 * kvw:2 * kvw + IDX_DIM])

    @pl.when(j == n_steps)
    def _():
        L = sc_scr.shape[1]
        tail = tail_ref[0]
        at = pl.ds(n_steps * MOBA_BLOCK, LANES)
        kv_scr[at, :] = tail[:, :2 * kvw].astype(BF16)
        sc_scr[:, at] = scores(tail[:, 2 * kvw:2 * kvw + IDX_DIM])
        row = lax.broadcasted_iota(jnp.int32, (TQ, 1), 0)
        q_pos = pos0 + row
        valid = lax.broadcasted_iota(jnp.int32, (1, L), 1) <= q_pos
        _dsa_select(sc_scr[...], valid, q_pos, row < t_new, key_scr, bias_scr, topk)
        bias = jnp.broadcast_to(bias_scr[...][None], (n_heads, TQ, L)).reshape(n_heads * TQ, L)
        s = _dot_bf16(qc_ref[0] * (D_HEAD ** -0.5), kv_scr[:, :kvw], _NT) + bias
        p = jnp.exp(s - jnp.max(s, axis=-1, keepdims=True))
        o = _dot_bf16(p, kv_scr[:, kvw:]) / jnp.sum(p, axis=-1, keepdims=True)
        grp = n_heads // C_KV_HEADS
        o_ref[0] = jnp.concatenate(
            [o[h * TQ:(h + 1) * TQ, (h // grp) * D_HEAD:(h // grp + 1) * D_HEAD] for h in range(n_heads)], axis=1)


def dsa_decode(qc, qi, wi, pages, layer, table, new_rows, pos0):
    B, T, W = qc.shape
    n_heads = W // D_HEAD
    n_pg = table.shape[1]
    assert n_pg % 2 == 0 and T <= SUBLANES and pos0 == n_pg * PAGE_SIZE
    n_steps = n_pg // 2
    last = n_steps - 1
    TQ = SUBLANES
    L = n_steps * MOBA_BLOCK + LANES
    kvw = C_KV_HEADS * D_HEAD
    grp = n_heads // C_KV_HEADS
    padq = lambda a: jnp.pad(a, ((0, 0), (0, TQ - T), (0, 0)))
    qi_rows = padq(qi).reshape(B, TQ * IDX_HEADS, IDX_DIM)
    wi_rows = jnp.broadcast_to((padq(wi) * IDX_HEADS ** -0.5).reshape(B, TQ * IDX_HEADS, 1),
                               (B, TQ * IDX_HEADS, LANES))
    qh = padq(qc).reshape(B, TQ, n_heads, D_HEAD).transpose(0, 2, 1, 3)
    z = jnp.zeros_like(qh)
    first = (jnp.arange(n_heads) < grp)[None, :, None, None]
    qc_rows = jnp.concatenate([jnp.where(first, qh, z), jnp.where(first, z, qh)], axis=-1)
    qc_rows = qc_rows.reshape(B, n_heads * TQ, kvw)
    tail = jnp.pad(new_rows, ((0, 0), (0, LANES - T), (0, 0)))

    def page_map(k):
        return lambda b, j, tbl: (layer, tbl[b, 2 * jnp.minimum(j, last) + k], 0, 0)

    R = pages.shape[3]
    fixed = lambda r, c: pl.BlockSpec((1, r, c), lambda b, j, tbl: (b, 0, 0))
    out = pl.pallas_call(
        functools.partial(_dsa_decode_kernel, n_steps=n_steps, t_new=T, topk=min(DSA_TOPK, (pos0 + T) // 4),
                          n_heads=n_heads, pos0=pos0),
        grid_spec=pltpu.PrefetchScalarGridSpec(
            num_scalar_prefetch=1,
            grid=(B, n_steps + 1),
            in_specs=[pl.BlockSpec((1, 1, PAGE_SIZE, R), page_map(0)),
                      pl.BlockSpec((1, 1, PAGE_SIZE, R), page_map(1)),
                      fixed(LANES, R), fixed(TQ * IDX_HEADS, IDX_DIM), fixed(TQ * IDX_HEADS, LANES),
                      fixed(n_heads * TQ, kvw)],
            out_specs=fixed(TQ, W),
            scratch_shapes=[pltpu.VMEM((L, 2 * kvw), BF16), pltpu.VMEM((TQ, L), F32),
                            pltpu.VMEM((TQ, L), jnp.int32), pltpu.VMEM((TQ, L), F32)]),
        out_shape=jax.ShapeDtypeStruct((B, TQ, W), F32),
        compiler_params=_params("parallel", "arbitrary"),
        name="dsa_decode",
    )(table, pages, pages, tail, qi_rows, wi_rows, qc_rows)
    return out[:, :T]


def dsa_attention(q3, drow, kv, pos0, n_keys):
    B, T, _ = q3.shape
    L, R = kv.shape[1], kv.shape[2]
    W = q3.shape[2] // 3
    tq = _row_tile(T, 128)
    topk = min(DSA_TOPK, n_keys // 4)
    return pl.pallas_call(
        functools.partial(_dsa_kernel, pos0=pos0, tq=tq, topk=topk, n_heads=W // D_HEAD),
        grid=(B, T // tq),
        in_specs=[pl.BlockSpec((1, tq, W), lambda b, i: (b, i, 1)),
                  pl.BlockSpec((1, tq, W), lambda b, i: (b, i, 2)),
                  pl.BlockSpec((1, tq, drow.shape[2]), lambda b, i: (b, i, 0)),
                  pl.BlockSpec((1, L, R), lambda b, i: (b, 0, 0))],
        out_specs=pl.BlockSpec((1, tq, W), lambda b, i: (b, i, 0)),
        out_shape=jax.ShapeDtypeStruct((B, T, W), F32),
        scratch_shapes=[pltpu.VMEM((tq, L), jnp.int32), pltpu.VMEM((tq, L), F32)],
        compiler_params=_params("parallel", "parallel"),
        name="dsa_attention",
    )(q3, q3, drow, kv)


def _segsum(x, ones_blk):
    hi = x.astype(BF16)
    r1 = x - hi.astype(F32)
    mid = r1.astype(BF16)
    lo = (r1 - mid.astype(F32)).astype(BF16)
    d = lambda a: jnp.dot(a, ones_blk, preferred_element_type=F32)
    return d(hi) + (d(mid) + d(lo))


def _rwkv_prep_kernel(z_ref, zs_ref, mu_ref, lora_ref, vec_ref, ones_ref, rs_ref, gb_ref, *t3_ref, width):
    W = width
    z = z_ref[0]
    xm = z + (zs_ref[0] - z) * mu_ref[...]
    r, k, v, g = (xm[:, n * W:(n + 1) * W] for n in range(4))
    lat = xm[:, 4 * W:4 * W + 2 * B_LORA]
    lane = lax.broadcasted_iota(jnp.int32, (1, 2 * B_LORA), 1)
    lat = jnp.where(lane < B_LORA, jnp.tanh(lat), lat)
    up = _dot3(lat, lora_ref[...], _NN)
    w0, a0, k_k, k_a, r_k = (vec_ref[n:n + 1, :] for n in range(5))
    wpre = -(w0 + up[:, :W])
    softplus = jnp.maximum(wpre, 0.0) + jnp.log(1.0 + jnp.exp(-jnp.abs(wpre)))
    log_decay = -jnp.exp(-softplus - 0.5)
    a = jax.nn.sigmoid(a0 + up[:, W:])
    ones_blk = ones_ref[...]
    kk = k * k_k
    kk = kk / jnp.maximum(jnp.sqrt(_segsum(kk * kk, ones_blk)), 1e-12)
    k = k * (1.0 + (a - 1.0) * k_a)
    kka = kk * a
    bonus = _segsum(r * k * r_k, ones_blk) * v
    rs_ref[0] = jnp.concatenate([r, jnp.exp(log_decay), k, kk, kka, v, log_decay], axis=1)
    gb_ref[0] = jnp.concatenate([jax.nn.sigmoid(g), bonus], axis=1)
    if t3_ref:
        t3_ref[0][0] = jnp.concatenate([log_decay.T, kka.T, k.T], axis=0)


RS_SEGMENTS = 7


def rwkv_prep(zr, zs, mu, lora, vecs, ones_blk, key_major):
    B, T, C = zr.shape
    W = vecs.shape[1]
    tm = _row_tile(T, 256)
    full = lambda a: pl.BlockSpec(a.shape, lambda b, i: (0,) * a.ndim)
    row = lambda c: pl.BlockSpec((1, tm, c), lambda b, i: (b, i, 0))
    out_specs = [row(RS_SEGMENTS * W), row(2 * W)]
    out_shape = [jax.ShapeDtypeStruct((B, T, RS_SEGMENTS * W), F32), jax.ShapeDtypeStruct((B, T, 2 * W), F32)]
    if key_major:
        out_specs.append(pl.BlockSpec((1, 3 * W, tm), lambda b, i: (b, 0, i)))
        out_shape.append(jax.ShapeDtypeStruct((B, 3 * W, T), F32))
    return pl.pallas_call(
        functools.partial(_rwkv_prep_kernel, width=W),
        grid=(B, T // tm),
        in_specs=[row(C), row(C), full(mu), full(lora), full(vecs), full(ones_blk)],
        out_specs=out_specs,
        out_shape=out_shape,
        compiler_params=_params("parallel", "parallel"),
        name="rwkv_prep",
    )(zr, zs, mu, lora, vecs, ones_blk)


RWKV_CHUNK = 64


def _split(a):
    return _split_bf16(a)


def _dot3s(a, b, dims=_NN):
    dn = (dims, ((), ()))
    d = lambda x, y: lax.dot_general(x, y, dn, preferred_element_type=F32)
    return d(a[0], b[0]) + (d(a[0], b[1]) + d(a[1], b[0]))


def _rows(s, lo, hi):
    return s[0][lo:hi], s[1][lo:hi]


def _rwkv_chunk_kernel(rs_ref, t3_ref, gb_ref, s0_ref, ln_ref, y_ref, sn_ref, st_scr, *, bb, width):
    c = pl.program_id(1)
    W = width
    n_hp = W // LANES
    C = RWKV_CHUNK
    TL = 2 * C
    assert TL == LANES and C == D_HEAD

    @pl.when(c == 0)
    def _():
        st_scr[...] = s0_ref[...]

    ri = lax.broadcasted_iota(jnp.int32, (TL, TL), 0)
    ci = lax.broadcasted_iota(jnp.int32, (TL, TL), 1)
    same = (ri // C) == (ci // C)
    strict = same & (ci < ri)
    incl = same & (ci <= ri)
    low_blk = jnp.where(incl, 1.0, 0.0).astype(BF16)
    up_blk = jnp.where(same & (ri <= ci), 1.0, 0.0).astype(BF16)
    ones_blk = jnp.where(same, 1.0, 0.0).astype(BF16)
    eye = jnp.where(ri == ci, 1.0, 0.0)
    lane = lax.broadcasted_iota(jnp.int32, (1, TL), 1)
    left = lane < C
    zeros_c = jnp.zeros((C, TL), F32)

    def exact3(x, w, x_first):
        hi = x.astype(BF16)
        r1 = x - hi.astype(F32)
        mid = r1.astype(BF16)
        lo = (r1 - mid.astype(F32)).astype(BF16)
        d = (lambda a: jnp.dot(a, w, preferred_element_type=F32)) if x_first else \
            (lambda a: jnp.dot(w, a, preferred_element_type=F32))
        return d(hi) + (d(mid) + d(lo))

    pairs = []
    for p in range(bb * n_hp):
        b, hp = divmod(p, n_hp)
        seg = lambda n: rs_ref[b, :, pl.ds(n * W + hp * LANES, LANES)]
        r, k, kk, kka, v, lw = seg(0), seg(2), seg(3), seg(4), seg(5), seg(6)
        lw_t, kka_t, k_t = (t3_ref[b, pl.ds(n * W + hp * LANES, LANES), :] for n in range(3))
        cum = exact3(lw, low_blk, False)
        cum_t = exact3(lw_t, up_blk, True)
        a_til = -kk * jnp.exp(cum - lw)
        r_til = r * jnp.exp(cum)
        inv = jnp.exp(-cum)
        bk = _split(jnp.concatenate([kka * inv, k * inv], axis=0))
        cend = jnp.where(left, cum_t[:, C - 1:C], cum_t[:, TL - 1:TL])
        rest = jnp.exp(cend - cum_t)
        d = dict(b=b, hp=hp, v=v, vs=_split(v), a_til=a_til, r_til=r_til, cum_t=cum_t,
                 bk_hat_t=jnp.concatenate([kka_t * rest, k_t * rest], axis=1),
                 lak=[], mrbk=[], x=[], t=[])
        for h in range(2):
            mh = (lane // C) == h
            ar = _split(jnp.concatenate([jnp.where(mh, a_til, 0.0), jnp.where(mh, r_til, 0.0)], axis=0))
            m = _dot3s(ar, bk, _NT)
            lab = jnp.where(strict, m[:TL, :TL], 0.0)
            d["lak"].append(_split(jnp.where(strict, m[:TL, TL:], 0.0)))
            d["mrbk"].append(_split(jnp.concatenate([jnp.where(incl, m[TL:, :TL], 0.0),
                                                     jnp.where(incl, m[TL:, TL:], 0.0)], axis=1)))
            d["x"].append(_split(lab))
            d["t"].append(eye + lab)
        pairs.append(d)

    for _ in range(5):
        for d in pairs:
            for h in range(2):
                d["x"][h] = _split(_dot3s(d["x"][h], d["x"][h]))
                d["t"][h] = d["t"][h] + _dot3s(d["x"][h], _split(d["t"][h]))
    for p, d in enumerate(pairs):
        d["tinv"] = [_split(t) for t in d["t"]]
        d["st"] = st_scr[p]
        d["ys"] = []
        d["u_ext"] = None

    for s in range(2):
        lo, hi = s * C, (s + 1) * C
        in_chunk = (lax.broadcasted_iota(jnp.int32, (1, 2 * TL), 1) % TL) // C == s
        for d in pairs:
            lak, tinv, mrbk, vs = d["lak"], d["tinv"], d["mrbk"], d["vs"]
            sts = _split(d["st"])
            ars = _dot3s(_split(jnp.concatenate([d["a_til"][lo:hi], d["r_til"][lo:hi]], axis=0)), sts)
            rhs = ars[:C] + jnp.where(left, _dot3s(_rows(lak[0], lo, hi), vs), _dot3s(_rows(lak[1], lo, hi), vs))
            rhs_ext = _split(jnp.concatenate([rhs, zeros_c] if s == 0 else [zeros_c, rhs], axis=0))
            u = jnp.where(left, _dot3s(_rows(tinv[0], lo, hi), rhs_ext), _dot3s(_rows(tinv[1], lo, hi), rhs_ext))
            d["u_ext"] = jnp.concatenate([u, zeros_c] if s == 0 else [d["u_ext"][:C], u], axis=0)
            uv = _split(jnp.concatenate([d["u_ext"], d["v"]], axis=0))
            d["ys"].append(ars[C:] + jnp.where(left, _dot3s(_rows(mrbk[0], lo, hi), uv),
                                               _dot3s(_rows(mrbk[1], lo, hi), uv)))
            upd = _dot3s(_split(jnp.where(in_chunk, d["bk_hat_t"], 0.0)), uv)
            d["st"] = jnp.where(same, jnp.exp(d["cum_t"][:, hi - 1:hi]) * d["st"] + upd, 0.0)

    for p, d in enumerate(pairs):
        st_scr[p] = d["st"]
        b, hp = d["b"], d["hp"]
        y = jnp.concatenate(d["ys"], axis=0)
        mean = exact3(y, ones_blk, True) * (1.0 / C)
        var = exact3(jnp.square(y - mean), ones_blk, True) * (1.0 / C)
        cols = pl.ds(hp * LANES, LANES)
        yn = (y - mean) * lax.rsqrt(var + RWKV_GN_EPS) * ln_ref[0:1, cols] + ln_ref[1:2, cols]
        y_ref[b, :, cols] = (yn + gb_ref[b, :, pl.ds(W + hp * LANES, LANES)]) * gb_ref[b, :, cols]

    @pl.when(c == pl.num_programs(1) - 1)
    def _():
        sn_ref[...] = st_scr[...]


def rwkv_chunk_scan(rs, t3, gb, s0, ln_w, ln_b):
    B, T, _ = rs.shape
    W = gb.shape[2] // 2
    n_hp = W // LANES
    TL = 2 * RWKV_CHUNK
    bb = 2 if B % 2 == 0 else 1
    assert T % TL == 0
    ln = jnp.concatenate([ln_w.reshape(1, W), ln_b.reshape(1, W), jnp.zeros((SUBLANES - 2, W), F32)], axis=0)
    st_spec = pl.BlockSpec((bb * n_hp, LANES, LANES), lambda g, c: (g, 0, 0))
    return pl.pallas_call(
        functools.partial(_rwkv_chunk_kernel, bb=bb, width=W),
        grid=(B // bb, T // TL),
        in_specs=[pl.BlockSpec((bb, TL, RS_SEGMENTS * W), lambda g, c: (g, c, 0)),
                  pl.BlockSpec((bb, 3 * W, TL), lambda g, c: (g, 0, c)),
                  pl.BlockSpec((bb, TL, 2 * W), lambda g, c: (g, c, 0)),
                  st_spec,
                  pl.BlockSpec((SUBLANES, W), lambda g, c: (0, 0))],
        out_specs=[pl.BlockSpec((bb, TL, W), lambda g, c: (g, c, 0)), st_spec],
        out_shape=[jax.ShapeDtypeStruct((B, T, W), F32),
                   jax.ShapeDtypeStruct((B * n_hp, LANES, LANES), F32)],
        scratch_shapes=[pltpu.VMEM((bb * n_hp, LANES, LANES), F32)],
        compiler_params=_params("parallel", "arbitrary"),
        name="rwkv_chunk_scan",
    )(rs, t3, gb, s0, ln)


def _rwkv_scan_kernel(rs_ref, gb_ref, s0_ref, lnw_ref, lnb_ref, y_ref, sn_ref, s_scr, yt_scr,
                      *, bb, tc, t_total, width):
    c = pl.program_id(1)
    n_hp = width // LANES
    n_pair = bb * n_hp

    @pl.when(c == 0)
    def _():
        s_scr[...] = s0_ref[...]

    yt_scr[...] = jnp.zeros(yt_scr.shape, F32)
    lane = lax.broadcasted_iota(jnp.int32, (1, LANES), 1)
    left = lane < D_HEAD
    eye2 = jnp.where(lax.broadcasted_iota(jnp.int32, (D_HEAD, LANES), 0)
                     == lax.broadcasted_iota(jnp.int32, (D_HEAD, LANES), 1) % D_HEAD, 1.0, 0.0)
    tok = lax.broadcasted_iota(jnp.int32, (1, tc), 1)

    def halves(p):
        return (jnp.sum(jnp.where(left, p, 0.0), axis=-1, keepdims=True),
                jnp.sum(jnp.where(left, 0.0, p), axis=-1, keepdims=True))

    def step(g, carry):
        t0 = pl.multiple_of(g * SUBLANES, SUBLANES)
        for p in range(n_pair):
            b, hp = divmod(p, n_hp)
            grp = [rs_ref[b, pl.ds(t0, SUBLANES), pl.ds(n * width + hp * LANES, LANES)] for n in range(6)]
            s = s_scr[p]
            y_l, y_r = yt_scr[2 * p], yt_scr[2 * p + 1]
            for j in range(SUBLANES):
                r, w, k, kk, kka, v = (a[j:j + 1] for a in grp)
                sa_l, sa_r = halves(s * kk)
                v_l, v_r = halves(eye2 * v)
                s = s * w - jnp.where(left, sa_l, sa_r) * kka + jnp.where(left, v_l, v_r) * k
                o_l, o_r = halves(s * r)
                here = tok == t0 + j
                y_l = jnp.where(here, o_l, y_l)
                y_r = jnp.where(here, o_r, y_r)
            s_scr[p] = s
            yt_scr[2 * p], yt_scr[2 * p + 1] = y_l, y_r
        return carry

    n_tok = jnp.minimum(tc, t_total - c * tc)
    lax.fori_loop(0, (n_tok + SUBLANES - 1) // SUBLANES, step, 0)

    for p in range(n_pair):
        b, hp = divmod(p, n_hp)
        rows = []
        for h2 in range(2):
            y = yt_scr[2 * p + h2]
            mean = jnp.mean(y, axis=0, keepdims=True)
            var = jnp.mean(jnp.square(y - mean), axis=0, keepdims=True)
            seg = pl.ds(hp * LANES + h2 * D_HEAD, D_HEAD)
            rows.append((y - mean) * lax.rsqrt(var + RWKV_GN_EPS) * lnw_ref[seg, :] + lnb_ref[seg, :])
        yn = jnp.concatenate(rows, axis=0).T
        cols = pl.ds(hp * LANES, LANES)
        y_ref[b, :, cols] = (yn + gb_ref[b, :, pl.ds(width + hp * LANES, LANES)]) * gb_ref[b, :, cols]

    @pl.when(c == pl.num_programs(1) - 1)
    def _():
        sn_ref[...] = s_scr[...]


def rwkv_scan(rs, gb, s0, ln_w, ln_b, t_total):
    B, Tp, _ = rs.shape
    W = gb.shape[2] // 2
    n_hp = W // LANES
    tc = LANES
    bb = min(B, 4)
    assert B % bb == 0 and Tp % tc == 0
    lnw = jnp.broadcast_to(ln_w.reshape(W, 1), (W, tc))
    lnb = jnp.broadcast_to(ln_b.reshape(W, 1), (W, tc))
    return pl.pallas_call(
        functools.partial(_rwkv_scan_kernel, bb=bb, tc=tc, t_total=t_total, width=W),
        grid=(B // bb, Tp // tc),
        in_specs=[pl.BlockSpec((bb, tc, RS_SEGMENTS * W), lambda g, c: (g, c, 0)),
                  pl.BlockSpec((bb, tc, 2 * W), lambda g, c: (g, c, 0)),
                  pl.BlockSpec((bb * n_hp, D_HEAD, LANES), lambda g, c: (g, 0, 0)),
                  pl.BlockSpec((W, tc), lambda g, c: (0, 0)),
                  pl.BlockSpec((W, tc), lambda g, c: (0, 0))],
        out_specs=[pl.BlockSpec((bb, tc, W), lambda g, c: (g, c, 0)),
                   pl.BlockSpec((bb * n_hp, D_HEAD, LANES), lambda g, c: (g, 0, 0))],
        out_shape=[jax.ShapeDtypeStruct((B, Tp, W), F32),
                   jax.ShapeDtypeStruct((B * n_hp, D_HEAD, LANES), F32)],
        scratch_shapes=[pltpu.VMEM((bb * n_hp, D_HEAD, LANES), F32),
                        pltpu.VMEM((2 * bb * n_hp, D_HEAD, tc), F32)],
        compiler_params=_params("parallel", "arbitrary"),
        name="rwkv_scan",
    )(rs, gb, s0, lnw, lnb)


def _merge_kernel(ya_ref, yb_ref, yc_ref, gl_ref, x_ref, gt_ref, wb_ref, wo_ref, o_ref):
    D = x_ref.shape[2]
    mix = None
    for n, y_ref in enumerate((ya_ref, yb_ref, yc_ref)):
        proj = _dot_bf16(y_ref[0], wb_ref[n])
        gate = jax.nn.sigmoid(gl_ref[0, :, n * D:(n + 1) * D].astype(F32))
        mix = gate * proj if mix is None else mix + gate * proj
    o_ref[0] = x_ref[0] + gt_ref[0] * _dot_bf16(mix, wo_ref[...])


def merge(ya, yb, yc, gl, x, gt, wb, wo):
    B, T, D = x.shape
    W = ya.shape[2]
    tm = _row_tile(T, 256)
    tmod = tm if gt.shape[1] == T else 1
    mod_map = (lambda b, i: (b, i, 0)) if gt.shape[1] == T else (lambda b, i: (b, 0, 0))
    row = lambda c: pl.BlockSpec((1, tm, c), lambda b, i: (b, i, 0))
    return pl.pallas_call(
        _merge_kernel,
        grid=(B, T // tm),
        in_specs=[row(W), row(W), row(W), row(N_BRANCH * D), row(D), pl.BlockSpec((1, tmod, D), mod_map),
                  pl.BlockSpec(wb.shape, lambda b, i: (0, 0, 0)), pl.BlockSpec(wo.shape, lambda b, i: (0, 0))],
        out_specs=row(D),
        out_shape=jax.ShapeDtypeStruct((B, T, D), F32),
        compiler_params=_params("parallel", "parallel"),
        name="merge",
    )(ya, yb, yc, gl, x, gt, wb, wo)


def _top_values(x, n):
    vals = []
    for _ in range(n):
        m = jnp.max(x, axis=0, keepdims=True)
        vals.append(m)
        x = jnp.where(x == m, -jnp.inf, x)
    return vals


def _peer_route_kernel(x_ref, sc_ref, sh_ref, g_ref, wq_ref, keys_ref, ht_ref, s1_ref, s2_ref, e2_ref, st_ref):
    x = x_ref[0]
    h = x * lax.rsqrt(jnp.mean(x * x, axis=-1, keepdims=True) + EPS) * g_ref[...]
    h = h * (1.0 + sc_ref[0]) + sh_ref[0]
    ht_ref[...] = h.T.astype(BF16)
    q = _dot_bf16(h, wq_ref[...])
    tm = x.shape[0]
    for hd in range(PEER_HEADS):
        s1, s2 = (_dot_bf16(keys_ref[2 * hd + p], q[:, (2 * hd + p) * LANES:(2 * hd + p + 1) * LANES], _NT)
                  for p in range(2))
        t1 = _top_values(s1, PEER_TOPK)
        t2 = jnp.concatenate(_top_values(s2, PEER_TOPK), axis=0)
        best = _top_values(jnp.concatenate([a + t2 for a in t1], axis=0), PEER_TOPK)
        z = sum(jnp.exp(b - best[0]) for b in best)
        s1_ref[hd] = s1
        s2_ref[hd] = s2
        e2_ref[hd] = jnp.exp(s2 - t2[0:1])
        st_ref[hd] = jnp.concatenate([best[-1], t2[0:1] - best[0], 1.0 / z, jnp.zeros((SUBLANES - 3, tm), F32)],
                                     axis=0)


def peer_route(x, scale, shift, g, wq, keys):
    B, T, D = x.shape
    N = B * T
    tm = _row_tile(T, 256)
    nt = T // tm
    tmod = tm if scale.shape[1] == T else 1
    mod_map = (lambda b, i: (b, i, 0)) if scale.shape[1] == T else (lambda b, i: (b, 0, 0))
    hkn = jax.ShapeDtypeStruct((PEER_HEADS, PEER_KEYS, N), F32)
    hk_spec = pl.BlockSpec((PEER_HEADS, PEER_KEYS, tm), lambda b, i: (0, 0, b * nt + i))
    return pl.pallas_call(
        _peer_route_kernel,
        grid=(B, nt),
        in_specs=[pl.BlockSpec((1, tm, D), lambda b, i: (b, i, 0)),
                  pl.BlockSpec((1, tmod, D), mod_map), pl.BlockSpec((1, tmod, D), mod_map),
                  pl.BlockSpec((1, D), lambda b, i: (0, 0)),
                  pl.BlockSpec(wq.shape, lambda b, i: (0, 0)),
                  pl.BlockSpec(keys.shape, lambda b, i: (0, 0, 0))],
        out_specs=[pl.BlockSpec((D, tm), lambda b, i: (0, b * nt + i)), hk_spec, hk_spec, hk_spec,
                   pl.BlockSpec((PEER_HEADS, SUBLANES, tm), lambda b, i: (0, 0, b * nt + i))],
        out_shape=[jax.ShapeDtypeStruct((D, N), BF16), hkn, hkn, hkn,
                   jax.ShapeDtypeStruct((PEER_HEADS, SUBLANES, N), F32)],
        compiler_params=_params("parallel", "parallel"),
        name="peer_route",
    )(x, scale, shift, g.reshape(1, D), wq, keys)


def _peer_expert_kernel(ht_ref, s1_ref, s2_ref, e2_ref, st_ref, u_ref, vt_ref, x_ref, gt_ref, o_ref, acc_ref):
    e = pl.program_id(2)

    @pl.when(e == 0)
    def _():
        acc_ref[...] = jnp.zeros(acc_ref.shape, F32)

    ht = ht_ref[...]
    upd = None
    group = 2 * PEER_KEYS
    for gi in range(u_ref.shape[0] // group):
        act = jnp.dot(u_ref[gi * group:(gi + 1) * group, :], ht, preferred_element_type=F32)
        ws = []
        for r in range(2 * gi, 2 * gi + 2):
            gate = None
            for hd in range(PEER_HEADS):
                s1 = s1_ref[hd, r:r + 1, :]
                e1 = jnp.exp(s1 + st_ref[hd, 1:2, :]) * st_ref[hd, 2:3, :]
                gh = jnp.where(s2_ref[hd] + s1 >= st_ref[hd, 0:1, :], e2_ref[hd] * e1, 0.0)
                gate = gh if gate is None else gate + gh
            a = act[(r - 2 * gi) * PEER_KEYS:(r - 2 * gi + 1) * PEER_KEYS]
            gelu = 0.5 * a * (1.0 + lax.erf(a * (2.0 ** -0.5)))
            ws.append((gate * gelu).astype(BF16))
        d = jnp.dot(vt_ref[:, gi * group:(gi + 1) * group], jnp.concatenate(ws, axis=0),
                    preferred_element_type=F32)
        upd = d if upd is None else upd + d
    acc_ref[...] += upd

    @pl.when(e == pl.num_programs(2) - 1)
    def _():
        o_ref[0] = x_ref[0] + gt_ref[0] * acc_ref[...].T


def peer_expert(ht, s1, s2, e2, st, u, vt, x, gt):
    B, T, D = x.shape
    E = u.shape[0]
    tm = _row_tile(T, 256)
    nt = T // tm
    te = SUBLANES * PEER_KEYS
    tmod = tm if gt.shape[1] == T else 1
    mod_map = (lambda b, i, e: (b, i, 0)) if gt.shape[1] == T else (lambda b, i, e: (b, 0, 0))
    hk_spec = pl.BlockSpec((PEER_HEADS, PEER_KEYS, tm), lambda b, i, e: (0, 0, b * nt + i))
    return pl.pallas_call(
        _peer_expert_kernel,
        grid=(B, nt, E // te),
        in_specs=[pl.BlockSpec((D, tm), lambda b, i, e: (0, b * nt + i)),
                  pl.BlockSpec((PEER_HEADS, SUBLANES, tm), lambda b, i, e: (0, e, b * nt + i)),
                  hk_spec, hk_spec,
                  pl.BlockSpec((PEER_HEADS, SUBLANES, tm), lambda b, i, e: (0, 0, b * nt + i)),
                  pl.BlockSpec((te, D), lambda b, i, e: (e, 0)),
                  pl.BlockSpec((D, te), lambda b, i, e: (0, e)),
                  pl.BlockSpec((1, tm, D), lambda b, i, e: (b, i, 0)),
                  pl.BlockSpec((1, tmod, D), mod_map)],
        out_specs=pl.BlockSpec((1, tm, D), lambda b, i, e: (b, i, 0)),
        out_shape=jax.ShapeDtypeStruct((B, T, D), F32),
        scratch_shapes=[pltpu.VMEM((D, tm), F32)],
        compiler_params=_params("parallel", "parallel", "arbitrary"),
        name="peer_expert",
    )(ht, s1, s2, e2, st, u, vt, x, gt)


def _pack_state(s):
    B, H, dv, dk = s.shape
    return s.reshape(B, H // 2, 2, dv, dk).transpose(0, 1, 3, 2, 4).reshape(B * H // 2, dv, 2 * dk)


def _unpack_state(s, B):
    P, dv, dk2 = s.shape
    H = 2 * P // B
    return s.reshape(B, H // 2, dv, 2, dk2 // 2).transpose(0, 1, 3, 2, 4).reshape(B, H, dv, dk2 // 2)


def _pack_state_bd(s):
    B, H, dv, dk = s.shape
    st = s.transpose(0, 1, 3, 2).reshape(B, H // 2, 2, dk, dv)
    z = jnp.zeros((B, H // 2, dk, dv), s.dtype)
    top = jnp.concatenate([st[:, :, 0], z], axis=-1)
    bot = jnp.concatenate([z, st[:, :, 1]], axis=-1)
    return jnp.concatenate([top, bot], axis=-2).reshape(B * H // 2, 2 * dk, 2 * dv)


def _unpack_state_bd(s, B):
    P, dk2, dv2 = s.shape
    s = s.reshape(B, P // B, 2, dk2 // 2, 2, dv2 // 2)
    d = jnp.stack([s[:, :, 0, :, 0, :], s[:, :, 1, :, 1, :]], axis=2)
    return d.reshape(B, 2 * P // B, dk2 // 2, dv2 // 2).transpose(0, 1, 3, 2)


def _block(x, ada, pos0, past, shift_prev, state, lw, rowwise_mods):
    B, T, D = x.shape
    N = B * T
    sh1, sc1, gt1, sh2, sc2, gt2 = ada
    if rowwise_mods:
        xr = x.reshape(1, N, D)
        mod = lambda m: jnp.repeat(m, T, axis=0).reshape(1, N, D)
    else:
        xr = x
        mod = lambda m: m.reshape(B, 1, D)
    h = modulate(xr, mod(sc1), mod(sh1), lw["norm_mix"], BF16).reshape(N, D)
    q3 = matmul(h, lw["w_q3"])
    mrow = matmul(h, lw["w_moba"])
    zr = matmul(h, lw["w_rw"], tn_want=lw["w_rw"].shape[1])
    drow = matmul(h, lw["w_dsa"])
    gl = matmul(h, lw["w_gl"], out_dtype=BF16)
    MR = mrow.shape[1]
    RW = zr.shape[1]
    q3b = q3.reshape(B, T, -1)
    drow_b = drow.reshape(B, T, -1)

    if past is None:
        ident = jnp.arange(N // PAGE_SIZE, dtype=jnp.int32).reshape(B, T // PAGE_SIZE)
        kv_a, km = paged_cast(mrow.reshape(1, N // PAGE_SIZE, PAGE_SIZE, MR), 0, ident, None, MR // 2)
        kv_c = paged_cast(drow.reshape(1, N // PAGE_SIZE, PAGE_SIZE, drow.shape[1]), 0, ident, None)
        nbp = -(-km.shape[1] // 16) * 16
        km = jnp.pad(km, ((0, 0), (0, nbp - km.shape[1]), (0, 0)))
        ya = moba_attention(q3b, kv_a, km, pos0)
        yc = dsa_attention(q3b, drow_b, kv_c, pos0, pos0 + T)
    else:
        pages_a, pages_c, layer, table = past
        A = MR // 2
        ya = moba_decode(q3b[:, :, :A], pages_a, layer, table, mrow.reshape(B, T, MR))
        yc = dsa_decode(q3b[:, :, A:2 * A], q3b[:, :, 2 * A:], drow_b[:, :, DSA_ROW:DSA_ROW + IDX_HEADS],
                        pages_c, layer, table, drow_b[:, :, :DSA_ROW], pos0)

    zr_b = zr.reshape(B, T, RW)
    zs = jnp.concatenate([shift_prev[:, None], zr_b[:, :-1]], axis=1)
    prep_w = (lw["rwkv_mu"], lw["rwkv_lora"], lw["rwkv_vecs"], lw["ones_blk"])
    if T % (2 * RWKV_CHUNK) == 0:
        rs, gb, t3 = rwkv_prep(zr_b, zs, *prep_w, True)
        yb, s_new = rwkv_chunk_scan(rs, t3, gb, _pack_state_bd(state), lw["rwkv_ln_w"], lw["rwkv_ln_b"])
        s_new = _unpack_state_bd(s_new, B)
    else:
        rs, gb = rwkv_prep(zr_b.reshape(1, N, RW), zs.reshape(1, N, RW), *prep_w, False)
        Tp = -(-T // LANES) * LANES
        padt = lambda a: jnp.pad(a.reshape(B, T, -1), ((0, 0), (0, Tp - T), (0, 0)))
        A = gb.shape[2] // 2
        rs_p = padt(rs).at[:, T:, A:2 * A].set(1.0)
        yb, s_new = rwkv_scan(rs_p, padt(gb), _pack_state(state), lw["rwkv_ln_w"], lw["rwkv_ln_b"], T)
        yb = yb[:, :T]
        s_new = _unpack_state(s_new, B)

    W = ya.shape[2]
    rsh = (lambda a: a.reshape(1, N, -1)) if rowwise_mods else (lambda a: a.reshape(B, T, -1))
    x1 = merge(rsh(ya), rsh(yb), rsh(yc), rsh(gl), xr, mod(gt1), lw["w_branch"], lw["w_out"])
    ht, s1, s2, e2, st = peer_route(x1, mod(sc2), mod(sh2), lw["norm_ffn"], lw["peer_wq"], lw["peer_keys"])
    x2 = peer_expert(ht, s1, s2, e2, st, lw["peer_u"], lw["peer_vt"], x1, mod(gt2))
    return (x2.reshape(B, T, D), mrow.reshape(B, T, MR), drow_b[:, :, :DSA_ROW],
            s_new, zr_b[:, -1])


def kernel(x_prompt, x_sample, cache_moba, cache_dsa, state_rwkv, state_shift, page_table, c_prompt, c_sample, w_ada, b_ada, norm_mix, w_in, rwkv_mu, rwkv_w0, rwkv_w_up, rwkv_a0, rwkv_a_up, rwkv_k_k, rwkv_k_a, rwkv_r_k, rwkv_ln_w, rwkv_ln_b, w_branch, w_out, norm_ffn, peer_wq, peer_keys, peer_u, peer_v, norm_final):
    n_b, n_t, D = x_prompt.shape
    n_db, n_dt, _ = x_sample.shape
    depth = w_in.shape[0]
    past_len = page_table.shape[1] * PAGE_SIZE
    A = D // 2
    RW = 4 * A + 2 * B_LORA
    idx_w = IDX_HEADS * IDX_DIM
    sizes = (A, 2 * A, RW, A, idx_w, IDX_HEADS, DSA_ROW, N_BRANCH * D)
    off = np.concatenate([[0], np.cumsum(sizes)])
    assert off[-1] == w_in.shape[2]

    n_c = n_b + n_db
    n_cp = -(-n_c // SUBLANES) * SUBLANES
    c_all = jnp.pad(jnp.concatenate([c_prompt, c_sample], axis=0), ((0, n_cp - n_c), (0, 0)))
    ada_all = ada_matmul(c_all, w_ada, b_ada)

    ones_blk = jnp.asarray(np.kron(np.eye(A // D_HEAD), np.ones((D_HEAD, D_HEAD))), BF16)
    xp, xs = x_prompt, x_sample
    outs = [[] for _ in range(8)]
    for l in range(depth):
        wl = w_in[l]
        seg = lambda n: wl[:, off[n]:off[n + 1]]
        dsa_pad = LANES * (-(-(DSA_ROW + IDX_HEADS) // LANES)) - DSA_ROW - IDX_HEADS
        lora = jnp.zeros((2 * B_LORA, 2 * A), F32)
        lora = lora.at[:B_LORA, :A].set(rwkv_w_up[l]).at[B_LORA:, A:].set(rwkv_a_up[l])
        flat = lambda a: a.reshape(1, -1)
        lw = dict(
            norm_mix=norm_mix[l], norm_ffn=norm_ffn[l],
            w_q3=jnp.concatenate([seg(0), seg(3), seg(4)], axis=1).astype(BF16),
            w_moba=seg(1).astype(BF16), w_rw=seg(2).astype(BF16),
            w_dsa=jnp.pad(jnp.concatenate([seg(6), seg(5)], axis=1), ((0, 0), (0, dsa_pad))).astype(BF16),
            w_gl=seg(7).astype(BF16),
            rwkv_mu=flat(rwkv_mu[l]), rwkv_lora=lora, ones_blk=ones_blk,
            rwkv_vecs=jnp.concatenate([flat(rwkv_w0[l]), flat(rwkv_a0[l]), flat(rwkv_k_k[l]), flat(rwkv_k_a[l]),
                                       flat(rwkv_r_k[l]), jnp.zeros((SUBLANES - 5, A), F32)], axis=0),
            rwkv_ln_w=rwkv_ln_w[l], rwkv_ln_b=rwkv_ln_b[l],
            w_branch=w_branch[l].astype(BF16), w_out=w_out[l].astype(BF16),
            peer_wq=peer_wq[l].astype(BF16),
            peer_keys=peer_keys[l].reshape(2 * PEER_HEADS, PEER_KEYS, -1).astype(BF16),
            peer_u=peer_u[l].astype(BF16), peer_vt=transpose_cast(peer_v[l], BF16),
        )
        ada_p = jnp.split(ada_all[l, :n_b], 6, axis=-1)
        ada_s = jnp.split(ada_all[l, n_b:n_c], 6, axis=-1)
        xp, m_p, d_p, r_p, s_p = _block(
            xp, ada_p, 0, None, jnp.zeros((n_b, RW), F32),
            jnp.zeros((n_b, A // D_HEAD, D_HEAD, D_HEAD), F32), lw, False)
        xs, m_s, d_s, r_s, s_s = _block(
            xs, ada_s, past_len, (cache_moba, cache_dsa, l, page_table), state_shift[l], state_rwkv[l], lw, True)
        for lst, val in zip(outs, (m_p, m_s, d_p, d_s, r_p, r_s, s_p, s_s)):
            lst.append(val)
    y_prompt = rms_norm_rows(xp.reshape(n_b * n_t, D), norm_final).reshape(n_b, n_t, D)
    y_sample = rms_norm_rows(xs.reshape(n_db * n_dt, D), norm_final).reshape(n_db, n_dt, D)
    return (y_prompt, y_sample) + tuple(jnp.stack(o) for o in outs)
```

```python
import functools

import jax
import jax.numpy as jnp
import numpy as np
from jax import lax
from jax.experimental import pallas as pl
from jax.experimental.pallas import tpu as pltpu

D_HEAD = 64
MOBA_BLOCK = 256
MOBA_TOPK = 3
B_LORA = 64
C_KV_HEADS = 2
IDX_HEADS = 8
IDX_DIM = 64
DSA_TOPK = 256
N_BRANCH = 3
PEER_KEYS = 128
PEER_HEADS = 8
PEER_TOPK = 16
PAGE_SIZE = 128
EPS = 1e-6
RWKV_GN_EPS = 64e-5
NEG = -1e30

LANES = 128
SUBLANES = 8
VMEM_LIMIT_BYTES = 56 * 1024 * 1024

F32 = jnp.float32
BF16 = jnp.bfloat16
INT_MIN = -2 ** 31


def _params(*sem):
    return pltpu.CompilerParams(dimension_semantics=sem, vmem_limit_bytes=VMEM_LIMIT_BYTES)


def _split_bf16(a):
    hi = a.astype(BF16)
    lo = (a - hi.astype(F32)).astype(BF16)
    return hi, lo


def _dot3(a, b, dims):
    ah, al = _split_bf16(a)
    bh, bl = _split_bf16(b)
    dn = (dims, ((), ()))
    d = lambda x, y: lax.dot_general(x, y, dn, preferred_element_type=F32)
    return d(ah, bh) + (d(ah, bl) + d(al, bh))


_NN = ((1,), (0,))
_NT = ((1,), (1,))


def _dot_bf16(a, b, dims=_NN):
    return lax.dot_general(a.astype(BF16), b.astype(BF16), (dims, ((), ())), preferred_element_type=F32)


def _row_tile(n, want):
    t = min(n, want)
    while n % t:
        t //= 2
    return t


def _modulate_kernel(x_ref, sc_ref, sh_ref, g_ref, o_ref):
    x = x_ref[0]
    y = x * lax.rsqrt(jnp.mean(x * x, axis=-1, keepdims=True) + EPS) * g_ref[...]
    o_ref[0] = (y * (1.0 + sc_ref[0]) + sh_ref[0]).astype(o_ref.dtype)


def modulate(x, scale, shift, g, out_dtype):
    B, T, D = x.shape
    tm = _row_tile(T, 512)
    tmod = tm if scale.shape[1] == T else 1
    mod_map = (lambda b, i: (b, i, 0)) if scale.shape[1] == T else (lambda b, i: (b, 0, 0))
    return pl.pallas_call(
        _modulate_kernel,
        grid=(B, T // tm),
        in_specs=[pl.BlockSpec((1, tm, D), lambda b, i: (b, i, 0)),
                  pl.BlockSpec((1, tmod, D), mod_map),
                  pl.BlockSpec((1, tmod, D), mod_map),
                  pl.BlockSpec((1, D), lambda b, i: (0, 0))],
        out_specs=pl.BlockSpec((1, tm, D), lambda b, i: (b, i, 0)),
        out_shape=jax.ShapeDtypeStruct((B, T, D), out_dtype),
        compiler_params=_params("parallel", "parallel"),
        name="modulate",
    )(x, scale, shift, g.reshape(1, D))


def _rmsnorm_kernel(x_ref, g_ref, o_ref):
    x = x_ref[...]
    o_ref[...] = x * lax.rsqrt(jnp.mean(x * x, axis=-1, keepdims=True) + EPS) * g_ref[...]


def _transpose_cast_kernel(x_ref, o_ref):
    o_ref[...] = x_ref[...].T.astype(o_ref.dtype)


def transpose_cast(x, dtype):
    R, C = x.shape
    tr = _row_tile(R, 1024)
    return pl.pallas_call(
        _transpose_cast_kernel,
        grid=(R // tr,),
        in_specs=[pl.BlockSpec((tr, C), lambda i: (i, 0))],
        out_specs=pl.BlockSpec((C, tr), lambda i: (0, i)),
        out_shape=jax.ShapeDtypeStruct((C, R), dtype),
        compiler_params=_params("parallel"),
        name="transpose_cast",
    )(x)


def rms_norm_rows(x, g):
    N, D = x.shape
    tm = _row_tile(N, 512)
    return pl.pallas_call(
        _rmsnorm_kernel,
        grid=(N // tm,),
        in_specs=[pl.BlockSpec((tm, D), lambda i: (i, 0)), pl.BlockSpec((1, D), lambda i: (0, 0))],
        out_specs=pl.BlockSpec((tm, D), lambda i: (i, 0)),
        out_shape=jax.ShapeDtypeStruct((N, D), F32),
        compiler_params=_params("parallel"),
        name="final_norm",
    )(x, g.reshape(1, D))


def _mm_kernel(a_ref, w_ref, o_ref):
    o_ref[...] = jnp.dot(a_ref[...], w_ref[...], preferred_element_type=F32).astype(o_ref.dtype)


def _mm3_kernel(a_ref, w_ref, b_ref, o_ref):
    o_ref[0] = _dot3(a_ref[...], w_ref[0], _NN) + b_ref[0]


def matmul(a, w, out_dtype=F32, tn_want=1024):
    N, K = a.shape
    C = w.shape[1]
    tm = _row_tile(N, 512)
    tn = _row_tile(C, tn_want)
    return pl.pallas_call(
        _mm_kernel,
        grid=(C // tn, N // tm),
        in_specs=[pl.BlockSpec((tm, K), lambda j, i: (i, 0)), pl.BlockSpec((K, tn), lambda j, i: (0, j))],
        out_specs=pl.BlockSpec((tm, tn), lambda j, i: (i, j)),
        out_shape=jax.ShapeDtypeStruct((N, C), out_dtype),
        compiler_params=_params("parallel", "parallel"),
        name="matmul",
    )(a, w)


def ada_matmul(c, w, b):
    M, D = c.shape
    L, _, C = w.shape
    tn = _row_tile(C, 1536)
    return pl.pallas_call(
        _mm3_kernel,
        grid=(L, C // tn),
        in_specs=[pl.BlockSpec((M, D), lambda l, j: (0, 0)),
                  pl.BlockSpec((1, D, tn), lambda l, j: (l, 0, j)),
                  pl.BlockSpec((1, 1, tn), lambda l, j: (l, 0, j))],
        out_specs=pl.BlockSpec((1, M, tn), lambda l, j: (l, 0, j)),
        out_shape=jax.ShapeDtypeStruct((L, M, C), F32),
        compiler_params=_params("parallel", "parallel"),
        name="ada_matmul",
    )(c, w, b.reshape(L, 1, C))


def _paged_cast_kernel(tbl_ref, p0_ref, p1_ref, tail_ref, o_ref, *mean_ref, n_steps, mean_cols):
    j = pl.program_id(1)

    def emit(rows):
        o_ref[0] = rows.astype(o_ref.dtype)
        if mean_ref:
            mean_ref[0][0] = jnp.mean(rows[:, :mean_cols], axis=0, keepdims=True)

    @pl.when(j < n_steps)
    def _():
        emit(jnp.concatenate([p0_ref[0, 0], p1_ref[0, 0]], axis=0))

    @pl.when(j >= n_steps)
    def _():
        emit(tail_ref[0])


def paged_cast(pages, layer, table, tail, mean_cols=0):
    _, _, page, R = pages.shape
    B, n_pg = table.shape
    assert page == PAGE_SIZE and MOBA_BLOCK == 2 * PAGE_SIZE and n_pg % 2 == 0
    n_steps = n_pg // 2
    has_tail = tail is not None
    n_blk = n_steps + (1 if has_tail else 0)
    if not has_tail:
        tail = jnp.zeros((B, MOBA_BLOCK, R), F32)
    last = n_steps - 1

    def page_map(k):
        return lambda b, j, tbl: (layer, tbl[b, 2 * jnp.minimum(j, last) + k], 0, 0)

    out_shape = [jax.ShapeDtypeStruct((B, n_blk * MOBA_BLOCK, R), BF16)]
    out_specs = [pl.BlockSpec((1, MOBA_BLOCK, R), lambda b, j, tbl: (b, j, 0))]
    if mean_cols:
        out_shape.append(jax.ShapeDtypeStruct((B * n_blk, 1, mean_cols), F32))
        out_specs.append(pl.BlockSpec((1, 1, mean_cols), lambda b, j, tbl: (b * n_blk + j, 0, 0)))
    res = pl.pallas_call(
        functools.partial(_paged_cast_kernel, n_steps=n_steps, mean_cols=mean_cols),
        grid_spec=pltpu.PrefetchScalarGridSpec(
            num_scalar_prefetch=1,
            grid=(B, n_blk),
            in_specs=[pl.BlockSpec((1, 1, PAGE_SIZE, R), page_map(0)),
                      pl.BlockSpec((1, 1, PAGE_SIZE, R), page_map(1)),
                      pl.BlockSpec((1, MOBA_BLOCK, R), lambda b, j, tbl: (b, 0, 0))],
            out_specs=out_specs),
        out_shape=out_shape,
        compiler_params=_params("parallel", "arbitrary"),
        name="paged_cast",
    )(table, pages, pages, tail)
    if mean_cols:
        return res[0], res[1].reshape(B, n_blk, mean_cols)
    return res[0]


MOBA_CAUSAL_CLASSES = 8
DSA_CAUSAL_CLASSES = 4


def _prefix_lengths(L, n):
    while L % (n * MOBA_BLOCK):
        n //= 2
    return [L * (c + 1) // n for c in range(n)]


def _for_visible_prefix(L, n_classes, last_pos, body):
    prev = 0
    for lc in _prefix_lengths(L, n_classes):
        pl.when((last_pos >= prev) & (last_pos < lc))(functools.partial(body, lc))
        prev = lc


def _moba_kernel(q_ref, k_ref, v_ref, km_ref, o_ref, *, pos0, tq):
    i = pl.program_id(2)
    body = functools.partial(_moba_body, q_ref, k_ref, v_ref, km_ref, o_ref, pos0 + i * tq, tq)
    _for_visible_prefix(k_ref.shape[1], MOBA_CAUSAL_CLASSES, pos0 + (i + 1) * tq - 1, body)


def _moba_body(q_ref, k_ref, v_ref, km_ref, o_ref, q_pos0, tq, L):
    nbp = km_ref.shape[1]
    q = q_ref[0] * (D_HEAD ** -0.5)
    k = k_ref[0, :L]
    v = v_ref[0, :L]
    km = km_ref[0]
    lane = lax.broadcasted_iota(jnp.int32, (1, LANES), 1)
    q_pos = q_pos0 + lax.broadcasted_iota(jnp.int32, (tq, 1), 0)
    own = q_pos // MOBA_BLOCK
    key_pos = lax.broadcasted_iota(jnp.int32, (1, L), 1)
    causal = jnp.where(key_pos > q_pos, NEG, 0.0)
    blk_id = lax.broadcasted_iota(jnp.int32, (tq, nbp), 1).astype(F32)
    own = own.astype(F32)
    expand =jnp.where(lax.broadcasted_iota(jnp.int32, (nbp, L), 1) // MOBA_BLOCK
                       == lax.broadcasted_iota(jnp.int32, (nbp, L), 0), 1.0, 0.0).astype(BF16)
    past = blk_id < own
    outs = []
    for h in range(2):
        qh = jnp.where(lane // D_HEAD == h, q, 0.0)
        g = jnp.where(past, _dot3(qh, km, _NT), NEG)
        closed = jnp.where(blk_id == own, 0.0, NEG)
        for _ in range(MOBA_TOPK):
            m = jnp.max(g, axis=-1, keepdims=True)
            first = jnp.min(jnp.where(g == m, blk_id, nbp), axis=-1, keepdims=True)
            pick = blk_id == first
            closed = jnp.where(pick & past, 0.0, closed)
            g = jnp.where(pick, -jnp.inf, g)
        s = _dot_bf16(qh, k, _NT) + _dot_bf16(closed, expand) + causal
        p = jnp.exp(s - jnp.max(s, axis=-1, keepdims=True))
        o = _dot_bf16(p, v) / jnp.sum(p, axis=-1, keepdims=True)
        outs.append(o)
    o_ref[0] = jnp.where(lane // D_HEAD == 0, outs[0], outs[1])


def moba_attention(q3, kv, km, pos0):
    B, T, _ = q3.shape
    L = kv.shape[1]
    nbp = km.shape[1]
    tq = _row_tile(T, 128)
    n_pair = kv.shape[2] // (2 * LANES)
    return pl.pallas_call(
        functools.partial(_moba_kernel, pos0=pos0, tq=tq),
        grid=(B, n_pair, T // tq),
        in_specs=[pl.BlockSpec((1, tq, LANES), lambda b, hp, i: (b, i, hp)),
                  pl.BlockSpec((1, L, LANES), lambda b, hp, i: (b, 0, hp)),
                  pl.BlockSpec((1, L, LANES), lambda b, hp, i: (b, 0, n_pair + hp)),
                  pl.BlockSpec((1, nbp, LANES), lambda b, hp, i: (b, 0, hp))],
        out_specs=pl.BlockSpec((1, tq, LANES), lambda b, hp, i: (b, i, hp)),
        out_shape=jax.ShapeDtypeStruct((B, T, n_pair * LANES), F32),
        compiler_params=_params("parallel", "parallel", "parallel"),
        name="moba_attention",
    )(q3, kv, kv, km)


DSA_ROW = 2 * C_KV_HEADS * D_HEAD + IDX_DIM


def _dsa_kernel(qc_ref, qi_ref, wi_ref, kv_ref, o_ref, key_full, bias_full, *, pos0, tq, topk, n_heads):
    i = pl.program_id(1)
    body = functools.partial(_dsa_body, qc_ref, qi_ref, wi_ref, kv_ref, o_ref, key_full, bias_full,
                             pos0 + i * tq, tq, topk, n_heads)
    _for_visible_prefix(kv_ref.shape[1], DSA_CAUSAL_CLASSES, pos0 + (i + 1) * tq - 1, body)


def _dsa_body(qc_ref, qi_ref, wi_ref, kv_ref, o_ref, key_full, bias_full, q_pos0, tq, topk, n_heads, L):
    key_scr = key_full.at[:, :L]
    bias_scr = bias_full.at[:, :L]
    kvw = C_KV_HEADS * D_HEAD
    k2 = kv_ref[0, :L, 0:kvw]
    v2 = kv_ref[0, :L, kvw:2 * kvw]
    ki = kv_ref[0, :L, 2 * kvw:2 * kvw + IDX_DIM]
    qc = qc_ref[0] * (D_HEAD ** -0.5)
    qi = qi_ref[0]
    wi = wi_ref[0][:, DSA_ROW:DSA_ROW + IDX_HEADS] * (IDX_HEADS ** -0.5)
    q_pos = q_pos0 + lax.broadcasted_iota(jnp.int32, (tq, 1), 0)
    key_pos = lax.broadcasted_iota(jnp.int32, (1, L), 1)
    valid = key_pos <= q_pos

    score = jnp.zeros((tq, L), F32)
    for h in range(IDX_HEADS):
        lg = _dot_bf16(qi[:, h * IDX_DIM:(h + 1) * IDX_DIM], ki, _NT)
        score = score + jnp.maximum(lg, 0.0) * wi[:, h:h + 1]
    _dsa_select(score, valid, q_pos, None, key_scr, bias_scr, topk)

    bias = bias_scr[...]
    grp = n_heads // C_KV_HEADS
    zero = jnp.zeros((tq, D_HEAD), F32)
    outs = []
    for h in range(n_heads):
        g = h // grp
        qh = qc[:, h * D_HEAD:(h + 1) * D_HEAD]
        q128 = jnp.concatenate([qh, zero] if g == 0 else [zero, qh], axis=1)
        s = _dot_bf16(q128, k2, _NT) + bias
        p = jnp.exp(s - jnp.max(s, axis=-1, keepdims=True))
        o = _dot_bf16(p, v2) / jnp.sum(p, axis=-1, keepdims=True)
        outs.append(o[:, g * D_HEAD:(g + 1) * D_HEAD])
    o_ref[0] = jnp.concatenate(outs, axis=1)


def _dsa_select(score, valid, q_pos, row_ok, key_scr, bias_scr, topk):
    tq, L = score.shape
    score = jnp.where(valid, score, NEG)
    score = jnp.where(score == 0.0, 0.0, score)
    bits = pltpu.bitcast(score, jnp.int32)
    key = jnp.where(bits < 0, bits ^ 0x7FFFFFFF, bits)
    key_scr[...] = key

    def count_ge(c):
        return jnp.sum(jnp.where(key_scr[...] >= c, 1.0, 0.0), axis=-1, keepdims=True)

    t0 = jnp.where(count_ge(jnp.zeros((tq, 1), jnp.int32)) >= topk, 0, INT_MIN).astype(jnp.int32)

    def bit_step(it, t):
        c = t | lax.shift_left(jnp.int32(1), 30 - it)
        return jnp.where(count_ge(c) >= topk, c, t)

    thr = lax.fori_loop(0, 31, bit_step, t0)
    key = key_scr[...]
    take = (key >= thr) & valid
    bias_scr[...] = jnp.where(take, 0.0, NEG)
    n_take = jnp.sum(jnp.where(take, 1.0, 0.0), axis=-1, keepdims=True)
    if row_ok is not None:
        n_take = jnp.where(row_ok, n_take, 0.0)

    @pl.when(jnp.max(n_take) > topk)
    def _():
        above = (key > thr) & valid
        need = topk - jnp.sum(jnp.where(above, 1.0, 0.0), axis=-1, keepdims=True)
        upper = jnp.where(lax.broadcasted_iota(jnp.int32, (LANES, LANES), 0)
                          < lax.broadcasted_iota(jnp.int32, (LANES, LANES), 1), 1.0, 0.0).astype(BF16)

        def blk(j, seen):
            sl = pl.ds(pl.multiple_of(j * LANES, LANES), LANES)
            kj = key_scr[:, sl]
            ok = (j * LANES + lax.broadcasted_iota(jnp.int32, (1, LANES), 1)) <= q_pos
            tied = jnp.where((kj == thr) & ok, 1.0, 0.0)
            before = seen + _dot_bf16(tied, upper)
            keep = ((kj > thr) & ok) | ((tied > 0.0) & (before < need))
            bias_scr[:, sl] = jnp.where(keep, 0.0, NEG)
            return seen + jnp.sum(tied, axis=-1, keepdims=True)

        lax.fori_loop(0, L // LANES, blk, jnp.zeros((tq, 1), F32))


def _moba_decode_kernel(tbl_ref, p0_ref, p1_ref, tail_ref, q_ref, o_ref, o_scr, m_scr, l_scr, g_scr,
                        *, n_steps, t_new, n_heads):
    j = pl.program_id(1)
    W = n_heads * D_HEAD
    R = q_ref.shape[1]
    lane = lax.broadcasted_iota(jnp.int32, (1, LANES), 1)
    head_cols = (lax.broadcasted_iota(jnp.int32, (R, W), 1) // D_HEAD
                 == lax.broadcasted_iota(jnp.int32, (R, W), 0) % n_heads)
    qm = jnp.where(head_cols, q_ref[0], 0.0)
    qs = qm * (D_HEAD ** -0.5)

    @pl.when(j == 0)
    def _():
        m_scr[...] = jnp.zeros(m_scr.shape, F32)
        l_scr[...] = jnp.zeros(l_scr.shape, F32)
        g_scr[...] = jnp.full(g_scr.shape, NEG, F32)

    @pl.when(j < n_steps)
    def _():
        rows = jnp.concatenate([p0_ref[0, 0], p1_ref[0, 0]], axis=0)
        k, v = rows[:, :W], rows[:, W:]
        gate = jnp.sum(qm * jnp.mean(k, axis=0, keepdims=True), axis=-1, keepdims=True)
        s = _dot_bf16(qs, k, _NT)
        m = jnp.max(s, axis=-1, keepdims=True)
        p = jnp.exp(s - m)
        o_scr[j] = _dot_bf16(p, v)
        here = lane == j
        m_scr[...] = jnp.where(here, m, m_scr[...])
        l_scr[...] = jnp.where(here, jnp.sum(p, axis=-1, keepdims=True), l_scr[...])
        g_scr[...] = jnp.where(here, gate, g_scr[...])

    @pl.when(j == n_steps)
    def _():
        tail = tail_ref[0]
        s = _dot_bf16(qs, tail[:, :W], _NT)
        n_t = tail.shape[0]
        seen = (lax.broadcasted_iota(jnp.int32, (R, n_t), 1)
                <= lax.broadcasted_iota(jnp.int32, (R, n_t), 0) // n_heads)
        s = jnp.where(seen, s, NEG)
        m_own = jnp.max(s, axis=-1, keepdims=True)
        p = jnp.exp(s - m_own)
        l_own = jnp.sum(p, axis=-1, keepdims=True)
        o_own = _dot_bf16(p, tail[:, W:])
        blk = lane.astype(F32)
        g = g_scr[...]
        sel = jnp.zeros(g.shape, F32)
        for _ in range(MOBA_TOPK):
            mx = jnp.max(g, axis=-1, keepdims=True)
            first = jnp.min(jnp.where(g == mx, blk, float(LANES)), axis=-1, keepdims=True)
            pick = blk == first
            sel = jnp.where(pick & (lane < n_steps), 1.0, sel)
            g = jnp.where(pick, -jnp.inf, g)
        m_all = jnp.maximum(m_own, jnp.max(jnp.where(sel > 0.0, m_scr[...], -jnp.inf), axis=-1, keepdims=True))
        wgt = jnp.where(sel > 0.0, jnp.exp(m_scr[...] - m_all), 0.0)
        w_own = jnp.exp(m_own - m_all)
        denom = jnp.sum(wgt * l_scr[...], axis=-1, keepdims=True) + w_own * l_own

        def add_block(jj, acc):
            return acc + jnp.sum(jnp.where(lane == jj, wgt, 0.0), axis=-1, keepdims=True) * o_scr[jj]

        acc = lax.fori_loop(0, n_steps, add_block, w_own * o_own)
        res = jnp.where(head_cols, acc / denom, 0.0)
        o_ref[0] = jnp.sum(res.reshape(R // n_heads, n_heads, W), axis=1)


def moba_decode(q, pages, layer, table, new_rows):
    B, T, W = q.shape
    n_heads = W // D_HEAD
    n_pg = table.shape[1]
    assert n_pg % 2 == 0 and n_pg // 2 <= LANES and T <= 16
    n_steps = n_pg // 2
    last = n_steps - 1
    R2 = pages.shape[3]
    q_rows = jnp.repeat(q, n_heads, axis=1)
    tail = jnp.pad(new_rows, ((0, 0), (0, 16 - T), (0, 0)))

    def page_map(k):
        return lambda b, j, tbl: (layer, tbl[b, 2 * jnp.minimum(j, last) + k], 0, 0)

    R = T * n_heads
    return pl.pallas_call(
        functools.partial(_moba_decode_kernel, n_steps=n_steps, t_new=T, n_heads=n_heads),
        grid_spec=pltpu.PrefetchScalarGridSpec(
            num_scalar_prefetch=1,
            grid=(B, n_steps + 1),
            in_specs=[pl.BlockSpec((1, 1, PAGE_SIZE, R2), page_map(0)),
                      pl.BlockSpec((1, 1, PAGE_SIZE, R2), page_map(1)),
                      pl.BlockSpec((1, 16, R2), lambda b, j, tbl: (b, 0, 0)),
                      pl.BlockSpec((1, R, W), lambda b, j, tbl: (b, 0, 0))],
            out_specs=pl.BlockSpec((1, T, W), lambda b, j, tbl: (b, 0, 0)),
            scratch_shapes=[pltpu.VMEM((n_steps, R, W), F32), pltpu.VMEM((R, LANES), F32),
                            pltpu.VMEM((R, LANES), F32), pltpu.VMEM((R, LANES), F32)]),
        out_shape=jax.ShapeDtypeStruct((B, T, W), F32),
        compiler_params=_params("parallel", "arbitrary"),
        name="moba_decode",
    )(table, pages, pages, tail, q_rows)


def _dsa_decode_kernel(tbl_ref, p0_ref, p1_ref, tail_ref, qi_ref, wi_ref, qc_ref, o_ref,
                       kv_scr, sc_scr, key_scr, bias_scr, *, n_steps, t_new, topk, n_heads, pos0):
    j = pl.program_id(1)
    kvw = C_KV_HEADS * D_HEAD
    TQ = SUBLANES

    def take(page_t, at):
        kv_scr[:, at] = page_t[:2 * kvw].astype(BF16)
        lg = _dot_bf16(qi_ref[0], page_t[2 * kvw:2 * kvw + IDX_DIM])
        sc = jnp.maximum(lg, 0.0) * wi_ref[0]
        sc_scr[:, at] = jnp.sum(sc.reshape(TQ, IDX_HEADS, PAGE_SIZE), axis=1)

    @pl.when(j < n_steps)
    def _():
        for k, p_ref in enumerate((p0_ref, p1_ref)):
            take(p_ref[0, 0], pl.ds(pl.multiple_of(j * MOBA_BLOCK + k * PAGE_SIZE, PAGE_SIZE), PAGE_SIZE))

    @pl.when(j == n_steps)
    def _():
        L = sc_scr.shape[1]
        take(tail_ref[0], pl.ds(n_steps * MOBA_BLOCK, PAGE_SIZE))
        row = lax.broadcasted_iota(jnp.int32, (TQ, 1), 0)
        q_pos = pos0 + row
        valid = lax.broadcasted_iota(jnp.int32, (1, L), 1) <= q_pos
        _dsa_select(sc_scr[...], valid, q_pos, row < t_new, key_scr, bias_scr, topk)
        bias = jnp.broadcast_to(bias_scr[...][None], (n_heads, TQ, L)).reshape(n_heads * TQ, L)
        s = _dot_bf16(qc_ref[0] * (D_HEAD ** -0.5), kv_scr[:kvw, :]) + bias
        p = jnp.exp(s - jnp.max(s, axis=-1, keepdims=True))
        o = _dot_bf16(p, kv_scr[kvw:, :], _NT) / jnp.sum(p, axis=-1, keepdims=True)
        grp = n_heads // C_KV_HEADS
        o_ref[0] = jnp.concatenate(
            [o[h * TQ:(h + 1) * TQ, (h // grp) * D_HEAD:(h // grp + 1) * D_HEAD] for h in range(n_heads)], axis=1)


def dsa_decode(qc, qi, wi, pages, layer, table, new_rows, pos0):
    B, T, W = qc.shape
    n_heads = W // D_HEAD
    n_pg = table.shape[1]
    assert n_pg % 2 == 0 and T <= SUBLANES and pos0 == n_pg * PAGE_SIZE
    n_steps = n_pg // 2
    last = n_steps - 1
    TQ = SUBLANES
    L = n_steps * MOBA_BLOCK + LANES
    kvw = C_KV_HEADS * D_HEAD
    grp = n_heads // C_KV_HEADS
    padq = lambda a: jnp.pad(a, ((0, 0), (0, TQ - T), (0, 0)))
    qi_rows = padq(qi).reshape(B, TQ * IDX_HEADS, IDX_DIM)
    wi_rows = jnp.broadcast_to((padq(wi) * IDX_HEADS ** -0.5).reshape(B, TQ * IDX_HEADS, 1),
                               (B, TQ * IDX_HEADS, LANES))
    qh = padq(qc).reshape(B, TQ, n_heads, D_HEAD).transpose(0, 2, 1, 3)
    z = jnp.zeros_like(qh)
    first = (jnp.arange(n_heads) < grp)[None, :, None, None]
    qc_rows = jnp.concatenate([jnp.where(first, qh, z), jnp.where(first, z, qh)], axis=-1)
    qc_rows = qc_rows.reshape(B, n_heads * TQ, kvw)
    pages = jnp.swapaxes(pages, 2, 3)
    tail = jnp.pad(jnp.swapaxes(new_rows, 1, 2), ((0, 0), (0, 0), (0, PAGE_SIZE - T)))

    def page_map(k):
        return lambda b, j, tbl: (layer, tbl[b, 2 * jnp.minimum(j, last) + k], 0, 0)

    R = pages.shape[2]
    fixed = lambda r, c: pl.BlockSpec((1, r, c), lambda b, j, tbl: (b, 0, 0))
    out = pl.pallas_call(
        functools.partial(_dsa_decode_kernel, n_steps=n_steps, t_new=T, topk=min(DSA_TOPK, (pos0 + T) // 4),
                          n_heads=n_heads, pos0=pos0),
        grid_spec=pltpu.PrefetchScalarGridSpec(
            num_scalar_prefetch=1,
            grid=(B, n_steps + 1),
            in_specs=[pl.BlockSpec((1, 1, R, PAGE_SIZE), page_map(0)),
                      pl.BlockSpec((1, 1, R, PAGE_SIZE), page_map(1)),
                      fixed(R, PAGE_SIZE), fixed(TQ * IDX_HEADS, IDX_DIM), fixed(TQ * IDX_HEADS, LANES),
                      fixed(n_heads * TQ, kvw)],
            out_specs=fixed(TQ, W),
            scratch_shapes=[pltpu.VMEM((2 * kvw, L), BF16), pltpu.VMEM((TQ, L), F32),
                            pltpu.VMEM((TQ, L), jnp.int32), pltpu.VMEM((TQ, L), F32)]),
        out_shape=jax.ShapeDtypeStruct((B, TQ, W), F32),
        compiler_params=_params("parallel", "arbitrary"),
        name="dsa_decode",
    )(table, pages, pages, tail, qi_rows, wi_rows, qc_rows)
    return out[:, :T]


def dsa_attention(q3, drow, kv, pos0, n_keys):
    B, T, _ = q3.shape
    L, R = kv.shape[1], kv.shape[2]
    W = q3.shape[2] // 3
    tq = _row_tile(T, 128)
    topk = min(DSA_TOPK, n_keys // 4)
    return pl.pallas_call(
        functools.partial(_dsa_kernel, pos0=pos0, tq=tq, topk=topk, n_heads=W // D_HEAD),
        grid=(B, T // tq),
        in_specs=[pl.BlockSpec((1, tq, W), lambda b, i: (b, i, 1)),
                  pl.BlockSpec((1, tq, W), lambda b, i: (b, i, 2)),
                  pl.BlockSpec((1, tq, drow.shape[2]), lambda b, i: (b, i, 0)),
                  pl.BlockSpec((1, L, R), lambda b, i: (b, 0, 0))],
        out_specs=pl.BlockSpec((1, tq, W), lambda b, i: (b, i, 0)),
        out_shape=jax.ShapeDtypeStruct((B, T, W), F32),
        scratch_shapes=[pltpu.VMEM((tq, L), jnp.int32), pltpu.VMEM((tq, L), F32)],
        compiler_params=_params("parallel", "parallel"),
        name="dsa_attention",
    )(q3, q3, drow, kv)


def _segsum(x, ones_blk):
    hi = x.astype(BF16)
    r1 = x - hi.astype(F32)
    mid = r1.astype(BF16)
    lo = (r1 - mid.astype(F32)).astype(BF16)
    d = lambda a: jnp.dot(a, ones_blk, preferred_element_type=F32)
    return d(hi) + (d(mid) + d(lo))


def _rwkv_prep_kernel(z_ref, zs_ref, mu_ref, lora_ref, vec_ref, ones_ref, rs_ref, gb_ref, *t3_ref, width):
    W = width
    z = z_ref[0]
    xm = z + (zs_ref[0] - z) * mu_ref[...]
    r, k, v, g = (xm[:, n * W:(n + 1) * W] for n in range(4))
    lat = xm[:, 4 * W:4 * W + 2 * B_LORA]
    lane = lax.broadcasted_iota(jnp.int32, (1, 2 * B_LORA), 1)
    lat = jnp.where(lane < B_LORA, jnp.tanh(lat), lat)
    up = _dot3(lat, lora_ref[...], _NN)
    w0, a0, k_k, k_a, r_k = (vec_ref[n:n + 1, :] for n in range(5))
    wpre = -(w0 + up[:, :W])
    softplus = jnp.maximum(wpre, 0.0) + jnp.log(1.0 + jnp.exp(-jnp.abs(wpre)))
    log_decay = -jnp.exp(-softplus - 0.5)
    a = jax.nn.sigmoid(a0 + up[:, W:])
    ones_blk = ones_ref[...]
    kk = k * k_k
    kk = kk / jnp.maximum(jnp.sqrt(_segsum(kk * kk, ones_blk)), 1e-12)
    k = k * (1.0 + (a - 1.0) * k_a)
    kka = kk * a
    bonus = _segsum(r * k * r_k, ones_blk) * v
    rs_ref[0] = jnp.concatenate([r, jnp.exp(log_decay), k, kk, kka, v, log_decay], axis=1)
    gb_ref[0] = jnp.concatenate([jax.nn.sigmoid(g), bonus], axis=1)
    if t3_ref:
        t3_ref[0][0] = jnp.concatenate([log_decay.T, kka.T, k.T], axis=0)


RS_SEGMENTS = 7


def rwkv_prep(zr, zs, mu, lora, vecs, ones_blk, key_major):
    B, T, C = zr.shape
    W = vecs.shape[1]
    tm = _row_tile(T, 256)
    full = lambda a: pl.BlockSpec(a.shape, lambda b, i: (0,) * a.ndim)
    row = lambda c: pl.BlockSpec((1, tm, c), lambda b, i: (b, i, 0))
    out_specs = [row(RS_SEGMENTS * W), row(2 * W)]
    out_shape = [jax.ShapeDtypeStruct((B, T, RS_SEGMENTS * W), F32), jax.ShapeDtypeStruct((B, T, 2 * W), F32)]
    if key_major:
        out_specs.append(pl.BlockSpec((1, 3 * W, tm), lambda b, i: (b, 0, i)))
        out_shape.append(jax.ShapeDtypeStruct((B, 3 * W, T), F32))
    return pl.pallas_call(
        functools.partial(_rwkv_prep_kernel, width=W),
        grid=(B, T // tm),
        in_specs=[row(C), row(C), full(mu), full(lora), full(vecs), full(ones_blk)],
        out_specs=out_specs,
        out_shape=out_shape,
        compiler_params=_params("parallel", "parallel"),
        name="rwkv_prep",
    )(zr, zs, mu, lora, vecs, ones_blk)


RWKV_CHUNK = 64


def _split(a):
    return _split_bf16(a)


def _dot3s(a, b, dims=_NN):
    dn = (dims, ((), ()))
    d = lambda x, y: lax.dot_general(x, y, dn, preferred_element_type=F32)
    return d(a[0], b[0]) + (d(a[0], b[1]) + d(a[1], b[0]))


def _rows(s, lo, hi):
    return s[0][lo:hi], s[1][lo:hi]


def _rwkv_chunk_kernel(rs_ref, t3_ref, gb_ref, s0_ref, ln_ref, y_ref, sn_ref, st_scr, *, bb, width):
    c = pl.program_id(1)
    W = width
    n_hp = W // LANES
    C = RWKV_CHUNK
    TL = 2 * C
    assert TL == LANES and C == D_HEAD

    @pl.when(c == 0)
    def _():
        st_scr[...] = s0_ref[...]

    ri = lax.broadcasted_iota(jnp.int32, (TL, TL), 0)
    ci = lax.broadcasted_iota(jnp.int32, (TL, TL), 1)
    same = (ri // C) == (ci // C)
    strict = same & (ci < ri)
    incl = same & (ci <= ri)
    low_blk = jnp.where(incl, 1.0, 0.0).astype(BF16)
    up_blk = jnp.where(same & (ri <= ci), 1.0, 0.0).astype(BF16)
    ones_blk = jnp.where(same, 1.0, 0.0).astype(BF16)
    eye = jnp.where(ri == ci, 1.0, 0.0)
    lane = lax.broadcasted_iota(jnp.int32, (1, TL), 1)
    left = lane < C
    zeros_c = jnp.zeros((C, TL), F32)

    def exact3(x, w, x_first):
        hi = x.astype(BF16)
        r1 = x - hi.astype(F32)
        mid = r1.astype(BF16)
        lo = (r1 - mid.astype(F32)).astype(BF16)
        d = (lambda a: jnp.dot(a, w, preferred_element_type=F32)) if x_first else \
            (lambda a: jnp.dot(w, a, preferred_element_type=F32))
        return d(hi) + (d(mid) + d(lo))

    pairs = []
    for p in range(bb * n_hp):
        b, hp = divmod(p, n_hp)
        seg = lambda n: rs_ref[b, :, pl.ds(n * W + hp * LANES, LANES)]
        r, k, kk, kka, v, lw = seg(0), seg(2), seg(3), seg(4), seg(5), seg(6)
        lw_t, kka_t, k_t = (t3_ref[b, pl.ds(n * W + hp * LANES, LANES), :] for n in range(3))
        cum = exact3(lw, low_blk, False)
        cum_t = exact3(lw_t, up_blk, True)
        a_til = -kk * jnp.exp(cum - lw)
        r_til = r * jnp.exp(cum)
        inv = jnp.exp(-cum)
        bk = _split(jnp.concatenate([kka * inv, k * inv], axis=0))
        cend = jnp.where(left, cum_t[:, C - 1:C], cum_t[:, TL - 1:TL])
        rest = jnp.exp(cend - cum_t)
        d = dict(b=b, hp=hp, v=v, vs=_split(v), a_til=a_til, r_til=r_til, cum_t=cum_t,
                 bk_hat_t=jnp.concatenate([kka_t * rest, k_t * rest], axis=1),
                 lak=[], mrbk=[], x=[], t=[])
        for h in range(2):
            mh = (lane // C) == h
            ar = _split(jnp.concatenate([jnp.where(mh, a_til, 0.0), jnp.where(mh, r_til, 0.0)], axis=0))
            m = _dot3s(ar, bk, _NT)
            lab = jnp.where(strict, m[:TL, :TL], 0.0)
            d["lak"].append(_split(jnp.where(strict, m[:TL, TL:], 0.0)))
            d["mrbk"].append(_split(jnp.concatenate([jnp.where(incl, m[TL:, :TL], 0.0),
                                                     jnp.where(incl, m[TL:, TL:], 0.0)], axis=1)))
            d["x"].append(_split(lab))
            d["t"].append(eye + lab)
        pairs.append(d)

    for _ in range(5):
        for d in pairs:
            for h in range(2):
                d["x"][h] = _split(_dot3s(d["x"][h], d["x"][h]))
                d["t"][h] = d["t"][h] + _dot3s(d["x"][h], _split(d["t"][h]))
    for p, d in enumerate(pairs):
        d["tinv"] = [_split(t) for t in d["t"]]
        d["st"] = st_scr[p]
        d["ys"] = []
        d["u_ext"] = None

    for s in range(2):
        lo, hi = s * C, (s + 1) * C
        in_chunk = (lax.broadcasted_iota(jnp.int32, (1, 2 * TL), 1) % TL) // C == s
        for d in pairs:
            lak, tinv, mrbk, vs = d["lak"], d["tinv"], d["mrbk"], d["vs"]
            sts = _split(d["st"])
            ars = _dot3s(_split(jnp.concatenate([d["a_til"][lo:hi], d["r_til"][lo:hi]], axis=0)), sts)
            rhs = ars[:C] + jnp.where(left, _dot3s(_rows(lak[0], lo, hi), vs), _dot3s(_rows(lak[1], lo, hi), vs))
            rhs_ext = _split(jnp.concatenate([rhs, zeros_c] if s == 0 else [zeros_c, rhs], axis=0))
            u = jnp.where(left, _dot3s(_rows(tinv[0], lo, hi), rhs_ext), _dot3s(_rows(tinv[1], lo, hi), rhs_ext))
            d["u_ext"] = jnp.concatenate([u, zeros_c] if s == 0 else [d["u_ext"][:C], u], axis=0)
            uv = _split(jnp.concatenate([d["u_ext"], d["v"]], axis=0))
            d["ys"].append(ars[C:] + jnp.where(left, _dot3s(_rows(mrbk[0], lo, hi), uv),
                                               _dot3s(_rows(mrbk[1], lo, hi), uv)))
            upd = _dot3s(_split(jnp.where(in_chunk, d["bk_hat_t"], 0.0)), uv)
            d["st"] = jnp.where(same, jnp.exp(d["cum_t"][:, hi - 1:hi]) * d["st"] + upd, 0.0)

    for p, d in enumerate(pairs):
        st_scr[p] = d["st"]
        b, hp = d["b"], d["hp"]
        y = jnp.concatenate(d["ys"], axis=0)
        mean = exact3(y, ones_blk, True) * (1.0 / C)
        var = exact3(jnp.square(y - mean), ones_blk, True) * (1.0 / C)
        cols = pl.ds(hp * LANES, LANES)
        yn = (y - mean) * lax.rsqrt(var + RWKV_GN_EPS) * ln_ref[0:1, cols] + ln_ref[1:2, cols]
        y_ref[b, :, cols] = (yn + gb_ref[b, :, pl.ds(W + hp * LANES, LANES)]) * gb_ref[b, :, cols]

    @pl.when(c == pl.num_programs(1) - 1)
    def _():
        sn_ref[...] = st_scr[...]


def rwkv_chunk_scan(rs, t3, gb, s0, ln_w, ln_b):
    B, T, _ = rs.shape
    W = gb.shape[2] // 2
    n_hp = W // LANES
    TL = 2 * RWKV_CHUNK
    bb = 2 if B % 2 == 0 else 1
    assert T % TL == 0
    ln = jnp.concatenate([ln_w.reshape(1, W), ln_b.reshape(1, W), jnp.zeros((SUBLANES - 2, W), F32)], axis=0)
    st_spec = pl.BlockSpec((bb * n_hp, LANES, LANES), lambda g, c: (g, 0, 0))
    return pl.pallas_call(
        functools.partial(_rwkv_chunk_kernel, bb=bb, width=W),
        grid=(B // bb, T // TL),
        in_specs=[pl.BlockSpec((bb, TL, RS_SEGMENTS * W), lambda g, c: (g, c, 0)),
                  pl.BlockSpec((bb, 3 * W, TL), lambda g, c: (g, 0, c)),
                  pl.BlockSpec((bb, TL, 2 * W), lambda g, c: (g, c, 0)),
                  st_spec,
                  pl.BlockSpec((SUBLANES, W), lambda g, c: (0, 0))],
        out_specs=[pl.BlockSpec((bb, TL, W), lambda g, c: (g, c, 0)), st_spec],
        out_shape=[jax.ShapeDtypeStruct((B, T, W), F32),
                   jax.ShapeDtypeStruct((B * n_hp, LANES, LANES), F32)],
        scratch_shapes=[pltpu.VMEM((bb * n_hp, LANES, LANES), F32)],
        compiler_params=_params("parallel", "arbitrary"),
        name="rwkv_chunk_scan",
    )(rs, t3, gb, s0, ln)


def _rwkv_scan_kernel(rs_ref, gb_ref, s0_ref, lnw_ref, lnb_ref, y_ref, sn_ref, s_scr, yt_scr,
                      *, bb, tc, t_total, width):
    c = pl.program_id(1)
    n_hp = width // LANES
    n_pair = bb * n_hp

    @pl.when(c == 0)
    def _():
        s_scr[...] = s0_ref[...]

    yt_scr[...] = jnp.zeros(yt_scr.shape, F32)
    lane = lax.broadcasted_iota(jnp.int32, (1, LANES), 1)
    left = lane < D_HEAD
    eye2 = jnp.where(lax.broadcasted_iota(jnp.int32, (D_HEAD, LANES), 0)
                     == lax.broadcasted_iota(jnp.int32, (D_HEAD, LANES), 1) % D_HEAD, 1.0, 0.0)
    tok = lax.broadcasted_iota(jnp.int32, (1, tc), 1)

    def halves(p):
        return (jnp.sum(jnp.where(left, p, 0.0), axis=-1, keepdims=True),
                jnp.sum(jnp.where(left, 0.0, p), axis=-1, keepdims=True))

    def step(g, carry):
        t0 = pl.multiple_of(g * SUBLANES, SUBLANES)
        for p in range(n_pair):
            b, hp = divmod(p, n_hp)
            grp = [rs_ref[b, pl.ds(t0, SUBLANES), pl.ds(n * width + hp * LANES, LANES)] for n in range(6)]
            s = s_scr[p]
            y_l, y_r = yt_scr[2 * p], yt_scr[2 * p + 1]
            for j in range(SUBLANES):
                r, w, k, kk, kka, v = (a[j:j + 1] for a in grp)
                sa_l, sa_r = halves(s * kk)
                v_l, v_r = halves(eye2 * v)
                s = s * w - jnp.where(left, sa_l, sa_r) * kka + jnp.where(left, v_l, v_r) * k
                o_l, o_r = halves(s * r)
                here = tok == t0 + j
                y_l = jnp.where(here, o_l, y_l)
                y_r = jnp.where(here, o_r, y_r)
            s_scr[p] = s
            yt_scr[2 * p], yt_scr[2 * p + 1] = y_l, y_r
        return carry

    n_tok = jnp.minimum(tc, t_total - c * tc)
    lax.fori_loop(0, (n_tok + SUBLANES - 1) // SUBLANES, step, 0)

    for p in range(n_pair):
        b, hp = divmod(p, n_hp)
        rows = []
        for h2 in range(2):
            y = yt_scr[2 * p + h2]
            mean = jnp.mean(y, axis=0, keepdims=True)
            var = jnp.mean(jnp.square(y - mean), axis=0, keepdims=True)
            seg = pl.ds(hp * LANES + h2 * D_HEAD, D_HEAD)
            rows.append((y - mean) * lax.rsqrt(var + RWKV_GN_EPS) * lnw_ref[seg, :] + lnb_ref[seg, :])
        yn = jnp.concatenate(rows, axis=0).T
        cols = pl.ds(hp * LANES, LANES)
        y_ref[b, :, cols] = (yn + gb_ref[b, :, pl.ds(width + hp * LANES, LANES)]) * gb_ref[b, :, cols]

    @pl.when(c == pl.num_programs(1) - 1)
    def _():
        sn_ref[...] = s_scr[...]


def rwkv_scan(rs, gb, s0, ln_w, ln_b, t_total):
    B, Tp, _ = rs.shape
    W = gb.shape[2] // 2
    n_hp = W // LANES
    tc = LANES
    bb = min(B, 4)
    assert B % bb == 0 and Tp % tc == 0
    lnw = jnp.broadcast_to(ln_w.reshape(W, 1), (W, tc))
    lnb = jnp.broadcast_to(ln_b.reshape(W, 1), (W, tc))
    return pl.pallas_call(
        functools.partial(_rwkv_scan_kernel, bb=bb, tc=tc, t_total=t_total, width=W),
        grid=(B // bb, Tp // tc),
        in_specs=[pl.BlockSpec((bb, tc, RS_SEGMENTS * W), lambda g, c: (g, c, 0)),
                  pl.BlockSpec((bb, tc, 2 * W), lambda g, c: (g, c, 0)),
                  pl.BlockSpec((bb * n_hp, D_HEAD, LANES), lambda g, c: (g, 0, 0)),
                  pl.BlockSpec((W, tc), lambda g, c: (0, 0)),
                  pl.BlockSpec((W, tc), lambda g, c: (0, 0))],
        out_specs=[pl.BlockSpec((bb, tc, W), lambda g, c: (g, c, 0)),
                   pl.BlockSpec((bb * n_hp, D_HEAD, LANES), lambda g, c: (g, 0, 0))],
        out_shape=[jax.ShapeDtypeStruct((B, Tp, W), F32),
                   jax.ShapeDtypeStruct((B * n_hp, D_HEAD, LANES), F32)],
        scratch_shapes=[pltpu.VMEM((bb * n_hp, D_HEAD, LANES), F32),
                        pltpu.VMEM((2 * bb * n_hp, D_HEAD, tc), F32)],
        compiler_params=_params("parallel", "arbitrary"),
        name="rwkv_scan",
    )(rs, gb, s0, lnw, lnb)


def _merge_kernel(ya_ref, yb_ref, yc_ref, gl_ref, x_ref, gt_ref, wb_ref, wo_ref, o_ref):
    D = x_ref.shape[2]
    mix = None
    for n, y_ref in enumerate((ya_ref, yb_ref, yc_ref)):
        proj = _dot_bf16(y_ref[0], wb_ref[n])
        gate = jax.nn.sigmoid(gl_ref[0, :, n * D:(n + 1) * D].astype(F32))
        mix = gate * proj if mix is None else mix + gate * proj
    o_ref[0] = x_ref[0] + gt_ref[0] * _dot_bf16(mix, wo_ref[...])


def merge(ya, yb, yc, gl, x, gt, wb, wo):
    B, T, D = x.shape
    W = ya.shape[2]
    tm = _row_tile(T, 256)
    tmod = tm if gt.shape[1] == T else 1
    mod_map = (lambda b, i: (b, i, 0)) if gt.shape[1] == T else (lambda b, i: (b, 0, 0))
    row = lambda c: pl.BlockSpec((1, tm, c), lambda b, i: (b, i, 0))
    return pl.pallas_call(
        _merge_kernel,
        grid=(B, T // tm),
        in_specs=[row(W), row(W), row(W), row(N_BRANCH * D), row(D), pl.BlockSpec((1, tmod, D), mod_map),
                  pl.BlockSpec(wb.shape, lambda b, i: (0, 0, 0)), pl.BlockSpec(wo.shape, lambda b, i: (0, 0))],
        out_specs=row(D),
        out_shape=jax.ShapeDtypeStruct((B, T, D), F32),
        compiler_params=_params("parallel", "parallel"),
        name="merge",
    )(ya, yb, yc, gl, x, gt, wb, wo)


def _top_values(x, n):
    vals = []
    for _ in range(n):
        m = jnp.max(x, axis=0, keepdims=True)
        vals.append(m)
        x = jnp.where(x == m, -jnp.inf, x)
    return vals


def _peer_route_kernel(x_ref, sc_ref, sh_ref, g_ref, wq_ref, keys_ref, ht_ref, s1_ref, s2_ref, e2_ref, st_ref):
    x = x_ref[0]
    h = x * lax.rsqrt(jnp.mean(x * x, axis=-1, keepdims=True) + EPS) * g_ref[...]
    h = h * (1.0 + sc_ref[0]) + sh_ref[0]
    ht_ref[...] = h.T.astype(BF16)
    q = _dot_bf16(h, wq_ref[...])
    tm = x.shape[0]
    for hd in range(PEER_HEADS):
        s1, s2 = (_dot_bf16(keys_ref[2 * hd + p], q[:, (2 * hd + p) * LANES:(2 * hd + p + 1) * LANES], _NT)
                  for p in range(2))
        t1 = _top_values(s1, PEER_TOPK)
        t2 = jnp.concatenate(_top_values(s2, PEER_TOPK), axis=0)
        best = _top_values(jnp.concatenate([a + t2 for a in t1], axis=0), PEER_TOPK)
        z = sum(jnp.exp(b - best[0]) for b in best)
        s1_ref[hd] = s1
        s2_ref[hd] = s2
        e2_ref[hd] = jnp.exp(s2 - t2[0:1])
        st_ref[hd] = jnp.concatenate([best[-1], t2[0:1] - best[0], 1.0 / z, jnp.zeros((SUBLANES - 3, tm), F32)],
                                     axis=0)


def peer_route(x, scale, shift, g, wq, keys):
    B, T, D = x.shape
    N = B * T
    tm = _row_tile(T, 256)
    nt = T // tm
    tmod = tm if scale.shape[1] == T else 1
    mod_map = (lambda b, i: (b, i, 0)) if scale.shape[1] == T else (lambda b, i: (b, 0, 0))
    hkn = jax.ShapeDtypeStruct((PEER_HEADS, PEER_KEYS, N), F32)
    hk_spec = pl.BlockSpec((PEER_HEADS, PEER_KEYS, tm), lambda b, i: (0, 0, b * nt + i))
    return pl.pallas_call(
        _peer_route_kernel,
        grid=(B, nt),
        in_specs=[pl.BlockSpec((1, tm, D), lambda b, i: (b, i, 0)),
                  pl.BlockSpec((1, tmod, D), mod_map), pl.BlockSpec((1, tmod, D), mod_map),
                  pl.BlockSpec((1, D), lambda b, i: (0, 0)),
                  pl.BlockSpec(wq.shape, lambda b, i: (0, 0)),
                  pl.BlockSpec(keys.shape, lambda b, i: (0, 0, 0))],
        out_specs=[pl.BlockSpec((D, tm), lambda b, i: (0, b * nt + i)), hk_spec, hk_spec, hk_spec,
                   pl.BlockSpec((PEER_HEADS, SUBLANES, tm), lambda b, i: (0, 0, b * nt + i))],
        out_shape=[jax.ShapeDtypeStruct((D, N), BF16), hkn, hkn, hkn,
                   jax.ShapeDtypeStruct((PEER_HEADS, SUBLANES, N), F32)],
        compiler_params=_params("parallel", "parallel"),
        name="peer_route",
    )(x, scale, shift, g.reshape(1, D), wq, keys)


def _peer_expert_kernel(ht_ref, s1_ref, s2_ref, e2_ref, st_ref, u_ref, vt_ref, x_ref, gt_ref, o_ref, acc_ref):
    e = pl.program_id(2)

    @pl.when(e == 0)
    def _():
        acc_ref[...] = jnp.zeros(acc_ref.shape, F32)

    ht = ht_ref[...]
    upd = None
    group = 2 * PEER_KEYS
    for gi in range(u_ref.shape[0] // group):
        act = jnp.dot(u_ref[gi * group:(gi + 1) * group, :], ht, preferred_element_type=F32)
        ws = []
        for r in range(2 * gi, 2 * gi + 2):
            gate = None
            for hd in range(PEER_HEADS):
                s1 = s1_ref[hd, r:r + 1, :]
                e1 = jnp.exp(s1 + st_ref[hd, 1:2, :]) * st_ref[hd, 2:3, :]
                gh = jnp.where(s2_ref[hd] + s1 >= st_ref[hd, 0:1, :], e2_ref[hd] * e1, 0.0)
                gate = gh if gate is None else gate + gh
            a = act[(r - 2 * gi) * PEER_KEYS:(r - 2 * gi + 1) * PEER_KEYS]
            gelu = 0.5 * a * (1.0 + lax.erf(a * (2.0 ** -0.5)))
            ws.append((gate * gelu).astype(BF16))
        d = jnp.dot(vt_ref[:, gi * group:(gi + 1) * group], jnp.concatenate(ws, axis=0),
                    preferred_element_type=F32)
        upd = d if upd is None else upd + d
    acc_ref[...] += upd

    @pl.when(e == pl.num_programs(2) - 1)
    def _():
        o_ref[0] = x_ref[0] + gt_ref[0] * acc_ref[...].T


def peer_expert(ht, s1, s2, e2, st, u, vt, x, gt):
    B, T, D = x.shape
    E = u.shape[0]
    tm = _row_tile(T, 256)
    nt = T // tm
    te = SUBLANES * PEER_KEYS
    tmod = tm if gt.shape[1] == T else 1
    mod_map = (lambda b, i, e: (b, i, 0)) if gt.shape[1] == T else (lambda b, i, e: (b, 0, 0))
    hk_spec = pl.BlockSpec((PEER_HEADS, PEER_KEYS, tm), lambda b, i, e: (0, 0, b * nt + i))
    return pl.pallas_call(
        _peer_expert_kernel,
        grid=(B, nt, E // te),
        in_specs=[pl.BlockSpec((D, tm), lambda b, i, e: (0, b * nt + i)),
                  pl.BlockSpec((PEER_HEADS, SUBLANES, tm), lambda b, i, e: (0, e, b * nt + i)),
                  hk_spec, hk_spec,
                  pl.BlockSpec((PEER_HEADS, SUBLANES, tm), lambda b, i, e: (0, 0, b * nt + i)),
                  pl.BlockSpec((te, D), lambda b, i, e: (e, 0)),
                  pl.BlockSpec((D, te), lambda b, i, e: (0, e)),
                  pl.BlockSpec((1, tm, D), lambda b, i, e: (b, i, 0)),
                  pl.BlockSpec((1, tmod, D), mod_map)],
        out_specs=pl.BlockSpec((1, tm, D), lambda b, i, e: (b, i, 0)),
        out_shape=jax.ShapeDtypeStruct((B, T, D), F32),
        scratch_shapes=[pltpu.VMEM((D, tm), F32)],
        compiler_params=_params("parallel", "parallel", "arbitrary"),
        name="peer_expert",
    )(ht, s1, s2, e2, st, u, vt, x, gt)


def _pack_state(s):
    B, H, dv, dk = s.shape
    return s.reshape(B, H // 2, 2, dv, dk).transpose(0, 1, 3, 2, 4).reshape(B * H // 2, dv, 2 * dk)


def _unpack_state(s, B):
    P, dv, dk2 = s.shape
    H = 2 * P // B
    return s.reshape(B, H // 2, dv, 2, dk2 // 2).transpose(0, 1, 3, 2, 4).reshape(B, H, dv, dk2 // 2)


def _pack_state_bd(s):
    B, H, dv, dk = s.shape
    st = s.transpose(0, 1, 3, 2).reshape(B, H // 2, 2, dk, dv)
    z = jnp.zeros((B, H // 2, dk, dv), s.dtype)
    top = jnp.concatenate([st[:, :, 0], z], axis=-1)
    bot = jnp.concatenate([z, st[:, :, 1]], axis=-1)
    return jnp.concatenate([top, bot], axis=-2).reshape(B * H // 2, 2 * dk, 2 * dv)


def _unpack_state_bd(s, B):
    P, dk2, dv2 = s.shape
    s = s.reshape(B, P // B, 2, dk2 // 2, 2, dv2 // 2)
    d = jnp.stack([s[:, :, 0, :, 0, :], s[:, :, 1, :, 1, :]], axis=2)
    return d.reshape(B, 2 * P // B, dk2 // 2, dv2 // 2).transpose(0, 1, 3, 2)


def _block(x, ada, pos0, past, shift_prev, state, lw, rowwise_mods):
    B, T, D = x.shape
    N = B * T
    sh1, sc1, gt1, sh2, sc2, gt2 = ada
    if rowwise_mods:
        xr = x.reshape(1, N, D)
        mod = lambda m: jnp.repeat(m, T, axis=0).reshape(1, N, D)
    else:
        xr = x
        mod = lambda m: m.reshape(B, 1, D)
    h = modulate(xr, mod(sc1), mod(sh1), lw["norm_mix"], BF16).reshape(N, D)
    q3 = matmul(h, lw["w_q3"])
    mrow = matmul(h, lw["w_moba"])
    zr = matmul(h, lw["w_rw"], tn_want=lw["w_rw"].shape[1])
    drow = matmul(h, lw["w_dsa"])
    gl = matmul(h, lw["w_gl"], out_dtype=BF16)
    MR = mrow.shape[1]
    RW = zr.shape[1]
    q3b = q3.reshape(B, T, -1)
    drow_b = drow.reshape(B, T, -1)

    if past is None:
        ident = jnp.arange(N // PAGE_SIZE, dtype=jnp.int32).reshape(B, T // PAGE_SIZE)
        kv_a, km = paged_cast(mrow.reshape(1, N // PAGE_SIZE, PAGE_SIZE, MR), 0, ident, None, MR // 2)
        kv_c = paged_cast(drow.reshape(1, N // PAGE_SIZE, PAGE_SIZE, drow.shape[1]), 0, ident, None)
        nbp = -(-km.shape[1] // 16) * 16
        km = jnp.pad(km, ((0, 0), (0, nbp - km.shape[1]), (0, 0)))
        ya = moba_attention(q3b, kv_a, km, pos0)
        yc = dsa_attention(q3b, drow_b, kv_c, pos0, pos0 + T)
    else:
        pages_a, pages_c, layer, table = past
        A = MR // 2
        ya = moba_decode(q3b[:, :, :A], pages_a, layer, table, mrow.reshape(B, T, MR))
        yc = dsa_decode(q3b[:, :, A:2 * A], q3b[:, :, 2 * A:], drow_b[:, :, DSA_ROW:DSA_ROW + IDX_HEADS],
                        pages_c, layer, table, drow_b[:, :, :DSA_ROW], pos0)

    zr_b = zr.reshape(B, T, RW)
    zs = jnp.concatenate([shift_prev[:, None], zr_b[:, :-1]], axis=1)
    prep_w = (lw["rwkv_mu"], lw["rwkv_lora"], lw["rwkv_vecs"], lw["ones_blk"])
    if T % (2 * RWKV_CHUNK) == 0:
        rs, gb, t3 = rwkv_prep(zr_b, zs, *prep_w, True)
        yb, s_new = rwkv_chunk_scan(rs, t3, gb, _pack_state_bd(state), lw["rwkv_ln_w"], lw["rwkv_ln_b"])
        s_new = _unpack_state_bd(s_new, B)
    else:
        rs, gb = rwkv_prep(zr_b.reshape(1, N, RW), zs.reshape(1, N, RW), *prep_w, False)
        Tp = -(-T // LANES) * LANES
        padt = lambda a: jnp.pad(a.reshape(B, T, -1), ((0, 0), (0, Tp - T), (0, 0)))
        A = gb.shape[2] // 2
        rs_p = padt(rs).at[:, T:, A:2 * A].set(1.0)
        yb, s_new = rwkv_scan(rs_p, padt(gb), _pack_state(state), lw["rwkv_ln_w"], lw["rwkv_ln_b"], T)
        yb = yb[:, :T]
        s_new = _unpack_state(s_new, B)

    W = ya.shape[2]
    rsh = (lambda a: a.reshape(1, N, -1)) if rowwise_mods else (lambda a: a.reshape(B, T, -1))
    x1 = merge(rsh(ya), rsh(yb), rsh(yc), rsh(gl), xr, mod(gt1), lw["w_branch"], lw["w_out"])
    ht, s1, s2, e2, st = peer_route(x1, mod(sc2), mod(sh2), lw["norm_ffn"], lw["peer_wq"], lw["peer_keys"])
    x2 = peer_expert(ht, s1, s2, e2, st, lw["peer_u"], lw["peer_vt"], x1, mod(gt2))
    return (x2.reshape(B, T, D), mrow.reshape(B, T, MR), drow_b[:, :, :DSA_ROW],
            s_new, zr_b[:, -1])


def kernel(x_prompt, x_sample, cache_moba, cache_dsa, state_rwkv, state_shift, page_table, c_prompt, c_sample, w_ada, b_ada, norm_mix, w_in, rwkv_mu, rwkv_w0, rwkv_w_up, rwkv_a0, rwkv_a_up, rwkv_k_k, rwkv_k_a, rwkv_r_k, rwkv_ln_w, rwkv_ln_b, w_branch, w_out, norm_ffn, peer_wq, peer_keys, peer_u, peer_v, norm_final):
    n_b, n_t, D = x_prompt.shape
    n_db, n_dt, _ = x_sample.shape
    depth = w_in.shape[0]
    past_len = page_table.shape[1] * PAGE_SIZE
    A = D // 2
    RW = 4 * A + 2 * B_LORA
    idx_w = IDX_HEADS * IDX_DIM
    sizes = (A, 2 * A, RW, A, idx_w, IDX_HEADS, DSA_ROW, N_BRANCH * D)
    off = np.concatenate([[0], np.cumsum(sizes)])
    assert off[-1] == w_in.shape[2]

    n_c = n_b + n_db
    n_cp = -(-n_c // SUBLANES) * SUBLANES
    c_all = jnp.pad(jnp.concatenate([c_prompt, c_sample], axis=0), ((0, n_cp - n_c), (0, 0)))
    ada_all = ada_matmul(c_all, w_ada, b_ada)

    ones_blk = jnp.asarray(np.kron(np.eye(A // D_HEAD), np.ones((D_HEAD, D_HEAD))), BF16)
    xp, xs = x_prompt, x_sample
    outs = [[] for _ in range(8)]
    for l in range(depth):
        wl = w_in[l]
        seg = lambda n: wl[:, off[n]:off[n + 1]]
        dsa_pad = LANES * (-(-(DSA_ROW + IDX_HEADS) // LANES)) - DSA_ROW - IDX_HEADS
        lora = jnp.zeros((2 * B_LORA, 2 * A), F32)
        lora = lora.at[:B_LORA, :A].set(rwkv_w_up[l]).at[B_LORA:, A:].set(rwkv_a_up[l])
        flat = lambda a: a.reshape(1, -1)
        lw = dict(
            norm_mix=norm_mix[l], norm_ffn=norm_ffn[l],
            w_q3=jnp.concatenate([seg(0), seg(3), seg(4)], axis=1).astype(BF16),
            w_moba=seg(1).astype(BF16), w_rw=seg(2).astype(BF16),
            w_dsa=jnp.pad(jnp.concatenate([seg(6), seg(5)], axis=1), ((0, 0), (0, dsa_pad))).astype(BF16),
            w_gl=seg(7).astype(BF16),
            rwkv_mu=flat(rwkv_mu[l]), rwkv_lora=lora, ones_blk=ones_blk,
            rwkv_vecs=jnp.concatenate([flat(rwkv_w0[l]), flat(rwkv_a0[l]), flat(rwkv_k_k[l]), flat(rwkv_k_a[l]),
                                       flat(rwkv_r_k[l]), jnp.zeros((SUBLANES - 5, A), F32)], axis=0),
            rwkv_ln_w=rwkv_ln_w[l], rwkv_ln_b=rwkv_ln_b[l],
            w_branch=w_branch[l].astype(BF16), w_out=w_out[l].astype(BF16),
            peer_wq=peer_wq[l].astype(BF16),
            peer_keys=peer_keys[l].reshape(2 * PEER_HEADS, PEER_KEYS, -1).astype(BF16),
            peer_u=peer_u[l].astype(BF16), peer_vt=transpose_cast(peer_v[l], BF16),
        )
        ada_p = jnp.split(ada_all[l, :n_b], 6, axis=-1)
        ada_s = jnp.split(ada_all[l, n_b:n_c], 6, axis=-1)
        xp, m_p, d_p, r_p, s_p = _block(
            xp, ada_p, 0, None, jnp.zeros((n_b, RW), F32),
            jnp.zeros((n_b, A // D_HEAD, D_HEAD, D_HEAD), F32), lw, False)
        xs, m_s, d_s, r_s, s_s = _block(
            xs, ada_s, past_len, (cache_moba, cache_dsa, l, page_table), state_shift[l], state_rwkv[l], lw, True)
        for lst, val in zip(outs, (m_p, m_s, d_p, d_s, r_p, r_s, s_p, s_s)):
            lst.append(val)
    y_prompt = rms_norm_rows(xp.reshape(n_b * n_t, D), norm_final).reshape(n_b, n_t, D)
    y_sample = rms_norm_rows(xs.reshape(n_db * n_dt, D), norm_final).reshape(n_db, n_dt, D)
    return (y_prompt, y_sample) + tuple(jnp.stack(o) for o in outs)
```

```python
import functools

import jax
import jax.numpy as jnp
import numpy as np
from jax import lax
from jax.experimental import pallas as pl
from jax.experimental.pallas import tpu as pltpu

D_HEAD = 64
MOBA_BLOCK = 256
MOBA_TOPK = 3
B_LORA = 64
C_KV_HEADS = 2
IDX_HEADS = 8
IDX_DIM = 64
DSA_TOPK = 256
N_BRANCH = 3
PEER_KEYS = 128
PEER_HEADS = 8
PEER_TOPK = 16
PAGE_SIZE = 128
EPS = 1e-6
RWKV_GN_EPS = 64e-5
NEG = -1e30

LANES = 128
SUBLANES = 8
VMEM_LIMIT_BYTES = 56 * 1024 * 1024

F32 = jnp.float32
BF16 = jnp.bfloat16
INT_MIN = -2 ** 31


def _params(*sem):
    return pltpu.CompilerParams(dimension_semantics=sem, vmem_limit_bytes=VMEM_LIMIT_BYTES)


def _split_bf16(a):
    hi = a.astype(BF16)
    lo = (a - hi.astype(F32)).astype(BF16)
    return hi, lo


def _dot3(a, b, dims):
    ah, al = _split_bf16(a)
    bh, bl = _split_bf16(b)
    dn = (dims, ((), ()))
    d = lambda x, y: lax.dot_general(x, y, dn, preferred_element_type=F32)
    return d(ah, bh) + (d(ah, bl) + d(al, bh))


_NN = ((1,), (0,))
_NT = ((1,), (1,))


def _dot_bf16(a, b, dims=_NN):
    return lax.dot_general(a.astype(BF16), b.astype(BF16), (dims, ((), ())), preferred_element_type=F32)


def _row_tile(n, want):
    t = min(n, want)
    while n % t:
        t //= 2
    return t


def _modulate_kernel(x_ref, sc_ref, sh_ref, g_ref, o_ref):
    x = x_ref[0]
    y = x * lax.rsqrt(jnp.mean(x * x, axis=-1, keepdims=True) + EPS) * g_ref[...]
    o_ref[0] = (y * (1.0 + sc_ref[0]) + sh_ref[0]).astype(o_ref.dtype)


def modulate(x, scale, shift, g, out_dtype):
    B, T, D = x.shape
    tm = _row_tile(T, 512)
    tmod = tm if scale.shape[1] == T else 1
    mod_map = (lambda b, i: (b, i, 0)) if scale.shape[1] == T else (lambda b, i: (b, 0, 0))
    return pl.pallas_call(
        _modulate_kernel,
        grid=(B, T // tm),
        in_specs=[pl.BlockSpec((1, tm, D), lambda b, i: (b, i, 0)),
                  pl.BlockSpec((1, tmod, D), mod_map),
                  pl.BlockSpec((1, tmod, D), mod_map),
                  pl.BlockSpec((1, D), lambda b, i: (0, 0))],
        out_specs=pl.BlockSpec((1, tm, D), lambda b, i: (b, i, 0)),
        out_shape=jax.ShapeDtypeStruct((B, T, D), out_dtype),
        compiler_params=_params("parallel", "parallel"),
        name="modulate",
    )(x, scale, shift, g.reshape(1, D))


def _rmsnorm_kernel(x_ref, g_ref, o_ref):
    x = x_ref[...]
    o_ref[...] = x * lax.rsqrt(jnp.mean(x * x, axis=-1, keepdims=True) + EPS) * g_ref[...]


def _transpose_cast_kernel(x_ref, o_ref):
    o_ref[...] = x_ref[...].T.astype(o_ref.dtype)


def transpose_cast(x, dtype):
    R, C = x.shape
    tr = _row_tile(R, 1024)
    return pl.pallas_call(
        _transpose_cast_kernel,
        grid=(R // tr,),
        in_specs=[pl.BlockSpec((tr, C), lambda i: (i, 0))],
        out_specs=pl.BlockSpec((C, tr), lambda i: (0, i)),
        out_shape=jax.ShapeDtypeStruct((C, R), dtype),
        compiler_params=_params("parallel"),
        name="transpose_cast",
    )(x)


def rms_norm_rows(x, g):
    N, D = x.shape
    tm = _row_tile(N, 512)
    return pl.pallas_call(
        _rmsnorm_kernel,
        grid=(N // tm,),
        in_specs=[pl.BlockSpec((tm, D), lambda i: (i, 0)), pl.BlockSpec((1, D), lambda i: (0, 0))],
        out_specs=pl.BlockSpec((tm, D), lambda i: (i, 0)),
        out_shape=jax.ShapeDtypeStruct((N, D), F32),
        compiler_params=_params("parallel"),
        name="final_norm",
    )(x, g.reshape(1, D))


def _mm_kernel(a_ref, w_ref, o_ref):
    o_ref[...] = jnp.dot(a_ref[...], w_ref[...], preferred_element_type=F32).astype(o_ref.dtype)


def _mm3_kernel(a_ref, w_ref, b_ref, o_ref):
    o_ref[0] = _dot3(a_ref[...], w_ref[0], _NN) + b_ref[0]


def matmul(a, w, out_dtype=F32, tn_want=1024):
    N, K = a.shape
    C = w.shape[1]
    tm = _row_tile(N, 512)
    tn = _row_tile(C, tn_want)
    return pl.pallas_call(
        _mm_kernel,
        grid=(C // tn, N // tm),
        in_specs=[pl.BlockSpec((tm, K), lambda j, i: (i, 0)), pl.BlockSpec((K, tn), lambda j, i: (0, j))],
        out_specs=pl.BlockSpec((tm, tn), lambda j, i: (i, j)),
        out_shape=jax.ShapeDtypeStruct((N, C), out_dtype),
        compiler_params=_params("parallel", "parallel"),
        name="matmul",
    )(a, w)


def ada_matmul(c, w, b):
    M, D = c.shape
    L, _, C = w.shape
    tn = _row_tile(C, 1536)
    return pl.pallas_call(
        _mm3_kernel,
        grid=(L, C // tn),
        in_specs=[pl.BlockSpec((M, D), lambda l, j: (0, 0)),
                  pl.BlockSpec((1, D, tn), lambda l, j: (l, 0, j)),
                  pl.BlockSpec((1, 1, tn), lambda l, j: (l, 0, j))],
        out_specs=pl.BlockSpec((1, M, tn), lambda l, j: (l, 0, j)),
        out_shape=jax.ShapeDtypeStruct((L, M, C), F32),
        compiler_params=_params("parallel", "parallel"),
        name="ada_matmul",
    )(c, w, b.reshape(L, 1, C))


def _paged_cast_kernel(tbl_ref, p0_ref, p1_ref, tail_ref, o_ref, *mean_ref, n_steps, mean_cols):
    j = pl.program_id(1)

    def emit(rows):
        o_ref[0] = rows.astype(o_ref.dtype)
        if mean_ref:
            mean_ref[0][0] = jnp.mean(rows[:, :mean_cols], axis=0, keepdims=True)

    @pl.when(j < n_steps)
    def _():
        emit(jnp.concatenate([p0_ref[0, 0], p1_ref[0, 0]], axis=0))

    @pl.when(j >= n_steps)
    def _():
        emit(tail_ref[0])


def paged_cast(pages, layer, table, tail, mean_cols=0):
    _, _, page, R = pages.shape
    B, n_pg = table.shape
    assert page == PAGE_SIZE and MOBA_BLOCK == 2 * PAGE_SIZE and n_pg % 2 == 0
    n_steps = n_pg // 2
    has_tail = tail is not None
    n_blk = n_steps + (1 if has_tail else 0)
    if not has_tail:
        tail = jnp.zeros((B, MOBA_BLOCK, R), F32)
    last = n_steps - 1

    def page_map(k):
        return lambda b, j, tbl: (layer, tbl[b, 2 * jnp.minimum(j, last) + k], 0, 0)

    out_shape = [jax.ShapeDtypeStruct((B, n_blk * MOBA_BLOCK, R), BF16)]
    out_specs = [pl.BlockSpec((1, MOBA_BLOCK, R), lambda b, j, tbl: (b, j, 0))]
    if mean_cols:
        out_shape.append(jax.ShapeDtypeStruct((B * n_blk, 1, mean_cols), F32))
        out_specs.append(pl.BlockSpec((1, 1, mean_cols), lambda b, j, tbl: (b * n_blk + j, 0, 0)))
    res = pl.pallas_call(
        functools.partial(_paged_cast_kernel, n_steps=n_steps, mean_cols=mean_cols),
        grid_spec=pltpu.PrefetchScalarGridSpec(
            num_scalar_prefetch=1,
            grid=(B, n_blk),
            in_specs=[pl.BlockSpec((1, 1, PAGE_SIZE, R), page_map(0)),
                      pl.BlockSpec((1, 1, PAGE_SIZE, R), page_map(1)),
                      pl.BlockSpec((1, MOBA_BLOCK, R), lambda b, j, tbl: (b, 0, 0))],
            out_specs=out_specs),
        out_shape=out_shape,
        compiler_params=_params("parallel", "arbitrary"),
        name="paged_cast",
    )(table, pages, pages, tail)
    if mean_cols:
        return res[0], res[1].reshape(B, n_blk, mean_cols)
    return res[0]


MOBA_CAUSAL_CLASSES = 8
DSA_CAUSAL_CLASSES = 4


def _prefix_lengths(L, n):
    while L % (n * MOBA_BLOCK):
        n //= 2
    return [L * (c + 1) // n for c in range(n)]


def _for_visible_prefix(L, n_classes, last_pos, body):
    prev = 0
    for lc in _prefix_lengths(L, n_classes):
        pl.when((last_pos >= prev) & (last_pos < lc))(functools.partial(body, lc))
        prev = lc


def _moba_kernel(q_ref, k_ref, v_ref, km_ref, o_ref, *, pos0, tq):
    i = pl.program_id(2)
    body = functools.partial(_moba_body, q_ref, k_ref, v_ref, km_ref, o_ref, pos0 + i * tq, tq)
    _for_visible_prefix(k_ref.shape[1], MOBA_CAUSAL_CLASSES, pos0 + (i + 1) * tq - 1, body)


def _moba_body(q_ref, k_ref, v_ref, km_ref, o_ref, q_pos0, tq, L):
    nbp = km_ref.shape[1]
    q = q_ref[0] * (D_HEAD ** -0.5)
    k = k_ref[0, :L]
    v = v_ref[0, :L]
    km = km_ref[0]
    lane = lax.broadcasted_iota(jnp.int32, (1, LANES), 1)
    q_pos = q_pos0 + lax.broadcasted_iota(jnp.int32, (tq, 1), 0)
    own = q_pos // MOBA_BLOCK
    key_pos = lax.broadcasted_iota(jnp.int32, (1, L), 1)
    causal = jnp.where(key_pos > q_pos, NEG, 0.0)
    blk_id = lax.broadcasted_iota(jnp.int32, (tq, nbp), 1).astype(F32)
    own = own.astype(F32)
    expand =jnp.where(lax.broadcasted_iota(jnp.int32, (nbp, L), 1) // MOBA_BLOCK
                       == lax.broadcasted_iota(jnp.int32, (nbp, L), 0), 1.0, 0.0).astype(BF16)
    past = blk_id < own
    outs = []
    for h in range(2):
        qh = jnp.where(lane // D_HEAD == h, q, 0.0)
        g = jnp.where(past, _dot3(qh, km, _NT), NEG)
        closed = jnp.where(blk_id == own, 0.0, NEG)
        for _ in range(MOBA_TOPK):
            m = jnp.max(g, axis=-1, keepdims=True)
            first = jnp.min(jnp.where(g == m, blk_id, nbp), axis=-1, keepdims=True)
            pick = blk_id == first
            closed = jnp.where(pick & past, 0.0, closed)
            g = jnp.where(pick, -jnp.inf, g)
        s = _dot_bf16(qh, k, _NT) + _dot_bf16(closed, expand) + causal
        p = jnp.exp(s - jnp.max(s, axis=-1, keepdims=True))
        o = _dot_bf16(p, v) / jnp.sum(p, axis=-1, keepdims=True)
        outs.append(o)
    o_ref[0] = jnp.where(lane // D_HEAD == 0, outs[0], outs[1])


def moba_attention(q3, kv, km, pos0):
    B, T, _ = q3.shape
    L = kv.shape[1]
    nbp = km.shape[1]
    tq = _row_tile(T, 128)
    n_pair = kv.shape[2] // (2 * LANES)
    return pl.pallas_call(
        functools.partial(_moba_kernel, pos0=pos0, tq=tq),
        grid=(B, n_pair, T // tq),
        in_specs=[pl.BlockSpec((1, tq, LANES), lambda b, hp, i: (b, i, hp)),
                  pl.BlockSpec((1, L, LANES), lambda b, hp, i: (b, 0, hp)),
                  pl.BlockSpec((1, L, LANES), lambda b, hp, i: (b, 0, n_pair + hp)),
                  pl.BlockSpec((1, nbp, LANES), lambda b, hp, i: (b, 0, hp))],
        out_specs=pl.BlockSpec((1, tq, LANES), lambda b, hp, i: (b, i, hp)),
        out_shape=jax.ShapeDtypeStruct((B, T, n_pair * LANES), F32),
        compiler_params=_params("parallel", "parallel", "parallel"),
        name="moba_attention",
    )(q3, kv, kv, km)


DSA_ROW = 2 * C_KV_HEADS * D_HEAD + IDX_DIM


def _dsa_kernel(qc_ref, qi_ref, wi_ref, kv_ref, o_ref, key_full, bias_full, *, pos0, tq, topk, n_heads):
    i = pl.program_id(1)
    body = functools.partial(_dsa_body, qc_ref, qi_ref, wi_ref, kv_ref, o_ref, key_full, bias_full,
                             pos0 + i * tq, tq, topk, n_heads)
    _for_visible_prefix(kv_ref.shape[1], DSA_CAUSAL_CLASSES, pos0 + (i + 1) * tq - 1, body)


def _dsa_body(qc_ref, qi_ref, wi_ref, kv_ref, o_ref, key_full, bias_full, q_pos0, tq, topk, n_heads, L):
    key_scr = key_full.at[:, :L]
    bias_scr = bias_full.at[:, :L]
    kvw = C_KV_HEADS * D_HEAD
    k2 = kv_ref[0, :L, 0:kvw]
    v2 = kv_ref[0, :L, kvw:2 * kvw]
    ki = kv_ref[0, :L, 2 * kvw:2 * kvw + IDX_DIM]
    qc = qc_ref[0] * (D_HEAD ** -0.5)
    qi = qi_ref[0]
    wi = wi_ref[0][:, DSA_ROW:DSA_ROW + IDX_HEADS] * (IDX_HEADS ** -0.5)
    q_pos = q_pos0 + lax.broadcasted_iota(jnp.int32, (tq, 1), 0)
    key_pos = lax.broadcasted_iota(jnp.int32, (1, L), 1)
    valid = key_pos <= q_pos

    score = jnp.zeros((tq, L), F32)
    for h in range(IDX_HEADS):
        lg = _dot_bf16(qi[:, h * IDX_DIM:(h + 1) * IDX_DIM], ki, _NT)
        score = score + jnp.maximum(lg, 0.0) * wi[:, h:h + 1]
    _dsa_select(score, valid, q_pos, None, key_scr, bias_scr, topk)

    bias = bias_scr[...]
    grp = n_heads // C_KV_HEADS
    zero = jnp.zeros((tq, D_HEAD), F32)
    outs = []
    for h in range(n_heads):
        g = h // grp
        qh = qc[:, h * D_HEAD:(h + 1) * D_HEAD]
        q128 = jnp.concatenate([qh, zero] if g == 0 else [zero, qh], axis=1)
        s = _dot_bf16(q128, k2, _NT) + bias
        p = jnp.exp(s - jnp.max(s, axis=-1, keepdims=True))
        o = _dot_bf16(p, v2) / jnp.sum(p, axis=-1, keepdims=True)
        outs.append(o[:, g * D_HEAD:(g + 1) * D_HEAD])
    o_ref[0] = jnp.concatenate(outs, axis=1)


def _dsa_select(score, valid, q_pos, row_ok, key_scr, bias_scr, topk):
    tq, L = score.shape
    score = jnp.where(valid, score, NEG)
    score = jnp.where(score == 0.0, 0.0, score)
    bits = pltpu.bitcast(score, jnp.int32)
    key = jnp.where(bits < 0, bits ^ 0x7FFFFFFF, bits)
    key_scr[...] = key

    def count_ge(c):
        return jnp.sum(jnp.where(key_scr[...] >= c, 1.0, 0.0), axis=-1, keepdims=True)

    t0 = jnp.where(count_ge(jnp.zeros((tq, 1), jnp.int32)) >= topk, 0, INT_MIN).astype(jnp.int32)

    def bit_step(it, t):
        c = t | lax.shift_left(jnp.int32(1), 30 - it)
        return jnp.where(count_ge(c) >= topk, c, t)

    thr = lax.fori_loop(0, 31, bit_step, t0)
    key = key_scr[...]
    take = (key >= thr) & valid
    bias_scr[...] = jnp.where(take, 0.0, NEG)
    n_take = jnp.sum(jnp.where(take, 1.0, 0.0), axis=-1, keepdims=True)
    if row_ok is not None:
        n_take = jnp.where(row_ok, n_take, 0.0)

    @pl.when(jnp.max(n_take) > topk)
    def _():
        above = (key > thr) & valid
        need = topk - jnp.sum(jnp.where(above, 1.0, 0.0), axis=-1, keepdims=True)
        upper = jnp.where(lax.broadcasted_iota(jnp.int32, (LANES, LANES), 0)
                          < lax.broadcasted_iota(jnp.int32, (LANES, LANES), 1), 1.0, 0.0).astype(BF16)

        def blk(j, seen):
            sl = pl.ds(pl.multiple_of(j * LANES, LANES), LANES)
            kj = key_scr[:, sl]
            ok = (j * LANES + lax.broadcasted_iota(jnp.int32, (1, LANES), 1)) <= q_pos
            tied = jnp.where((kj == thr) & ok, 1.0, 0.0)
            before = seen + _dot_bf16(tied, upper)
            keep = ((kj > thr) & ok) | ((tied > 0.0) & (before < need))
            bias_scr[:, sl] = jnp.where(keep, 0.0, NEG)
            return seen + jnp.sum(tied, axis=-1, keepdims=True)

        lax.fori_loop(0, L // LANES, blk, jnp.zeros((tq, 1), F32))


def _moba_decode_kernel(tbl_ref, *refs, n_steps, n_blocks, n_heads):
    per_step = n_blocks // n_steps
    page_refs = refs[:2 * per_step]
    tail_ref, q_ref, o_ref, o_scr, m_scr, l_scr, g_scr = refs[2 * per_step:]
    j = pl.program_id(1)
    W = n_heads * D_HEAD
    R = q_ref.shape[1]
    lane = lax.broadcasted_iota(jnp.int32, (1, LANES), 1)
    head_cols = (lax.broadcasted_iota(jnp.int32, (R, W), 1) // D_HEAD
                 == lax.broadcasted_iota(jnp.int32, (R, W), 0) % n_heads)
    qm = jnp.where(head_cols, q_ref[0], 0.0)
    qs = qm * (D_HEAD ** -0.5)

    @pl.when(j == 0)
    def _():
        m_scr[...] = jnp.zeros(m_scr.shape, F32)
        l_scr[...] = jnp.zeros(l_scr.shape, F32)
        g_scr[...] = jnp.full(g_scr.shape, NEG, F32)

    @pl.when(j < n_steps)
    def _():
        for i in range(per_step):
            jb = j * per_step + i
            rows = jnp.concatenate([page_refs[2 * i][0, 0], page_refs[2 * i + 1][0, 0]], axis=0)
            k, v = rows[:, :W], rows[:, W:]
            gate = jnp.sum(qm * jnp.mean(k, axis=0, keepdims=True), axis=-1, keepdims=True)
            s = _dot_bf16(qs, k, _NT)
            m = jnp.max(s, axis=-1, keepdims=True)
            p = jnp.exp(s - m)
            o_scr[jb] = _dot_bf16(p, v)
            here = lane == jb
            m_scr[...] = jnp.where(here, m, m_scr[...])
            l_scr[...] = jnp.where(here, jnp.sum(p, axis=-1, keepdims=True), l_scr[...])
            g_scr[...] = jnp.where(here, gate, g_scr[...])

    @pl.when(j == n_steps)
    def _():
        tail = tail_ref[0]
        s = _dot_bf16(qs, tail[:, :W], _NT)
        n_t = tail.shape[0]
        seen = (lax.broadcasted_iota(jnp.int32, (R, n_t), 1)
                <= lax.broadcasted_iota(jnp.int32, (R, n_t), 0) // n_heads)
        s = jnp.where(seen, s, NEG)
        m_own = jnp.max(s, axis=-1, keepdims=True)
        p = jnp.exp(s - m_own)
        l_own = jnp.sum(p, axis=-1, keepdims=True)
        o_own = _dot_bf16(p, tail[:, W:])
        blk = lane.astype(F32)
        g = g_scr[...]
        sel = jnp.zeros(g.shape, F32)
        for _ in range(MOBA_TOPK):
            mx = jnp.max(g, axis=-1, keepdims=True)
            first = jnp.min(jnp.where(g == mx, blk, float(LANES)), axis=-1, keepdims=True)
            pick = blk == first
            sel = jnp.where(pick & (lane < n_blocks), 1.0, sel)
            g = jnp.where(pick, -jnp.inf, g)
        m_all = jnp.maximum(m_own, jnp.max(jnp.where(sel > 0.0, m_scr[...], -jnp.inf), axis=-1, keepdims=True))
        wgt = jnp.where(sel > 0.0, jnp.exp(m_scr[...] - m_all), 0.0)
        w_own = jnp.exp(m_own - m_all)
        denom = jnp.sum(wgt * l_scr[...], axis=-1, keepdims=True) + w_own * l_own

        def add_block(jj, acc):
            return acc + jnp.sum(jnp.where(lane == jj, wgt, 0.0), axis=-1, keepdims=True) * o_scr[jj]

        acc = lax.fori_loop(0, n_blocks, add_block, w_own * o_own)
        res = jnp.where(head_cols, acc / denom, 0.0)
        o_ref[0] = jnp.sum(res.reshape(R // n_heads, n_heads, W), axis=1)


def moba_decode(q, pages, layer, table, new_rows):
    B, T, W = q.shape
    n_heads = W // D_HEAD
    n_pg = table.shape[1]
    assert n_pg % 2 == 0 and n_pg // 2 <= LANES and T <= 16
    n_blocks = n_pg // 2
    per_step = 2 if n_blocks % 2 == 0 else 1
    n_steps = n_blocks // per_step
    ppb = 2 * per_step
    last = n_steps - 1
    R2 = pages.shape[3]
    q_rows = jnp.repeat(q, n_heads, axis=1)
    tail = jnp.pad(new_rows, ((0, 0), (0, 16 - T), (0, 0)))

    def page_map(k):
        return lambda b, j, tbl: (layer, tbl[b, ppb * jnp.minimum(j, last) + k], 0, 0)

    R = T * n_heads
    return pl.pallas_call(
        functools.partial(_moba_decode_kernel, n_steps=n_steps, n_blocks=n_blocks, n_heads=n_heads),
        grid_spec=pltpu.PrefetchScalarGridSpec(
            num_scalar_prefetch=1,
            grid=(B, n_steps + 1),
            in_specs=[pl.BlockSpec((1, 1, PAGE_SIZE, R2), page_map(k)) for k in range(ppb)]
                     + [pl.BlockSpec((1, 16, R2), lambda b, j, tbl: (b, 0, 0)),
                        pl.BlockSpec((1, R, W), lambda b, j, tbl: (b, 0, 0))],
            out_specs=pl.BlockSpec((1, T, W), lambda b, j, tbl: (b, 0, 0)),
            scratch_shapes=[pltpu.VMEM((n_blocks, R, W), F32), pltpu.VMEM((R, LANES), F32),
                            pltpu.VMEM((R, LANES), F32), pltpu.VMEM((R, LANES), F32)]),
        out_shape=jax.ShapeDtypeStruct((B, T, W), F32),
        compiler_params=_params("parallel", "arbitrary"),
        name="moba_decode",
    )(table, *([pages] * ppb), tail, q_rows)


def _dsa_decode_kernel(tbl_ref, p0_ref, p1_ref, tail_ref, qi_ref, wi_ref, qc_ref, o_ref,
                       kv_scr, sc_scr, key_scr, bias_scr, *, n_steps, t_new, topk, n_heads, pos0):
    j = pl.program_id(1)
    kvw = C_KV_HEADS * D_HEAD
    TQ = SUBLANES

    def take(page_t, at):
        kv_scr[:, at] = page_t[:2 * kvw].astype(BF16)
        lg = _dot_bf16(qi_ref[0], page_t[2 * kvw:2 * kvw + IDX_DIM])
        sc = jnp.maximum(lg, 0.0) * wi_ref[0]
        sc_scr[:, at] = jnp.sum(sc.reshape(TQ, IDX_HEADS, PAGE_SIZE), axis=1)

    @pl.when(j < n_steps)
    def _():
        for k, p_ref in enumerate((p0_ref, p1_ref)):
            take(p_ref[0, 0], pl.ds(pl.multiple_of(j * MOBA_BLOCK + k * PAGE_SIZE, PAGE_SIZE), PAGE_SIZE))

    @pl.when(j == n_steps)
    def _():
        L = sc_scr.shape[1]
        take(tail_ref[0], pl.ds(n_steps * MOBA_BLOCK, PAGE_SIZE))
        row = lax.broadcasted_iota(jnp.int32, (TQ, 1), 0)
        q_pos = pos0 + row
        valid = lax.broadcasted_iota(jnp.int32, (1, L), 1) <= q_pos
        _dsa_select(sc_scr[...], valid, q_pos, row < t_new, key_scr, bias_scr, topk)
        bias = jnp.broadcast_to(bias_scr[...][None], (n_heads, TQ, L)).reshape(n_heads * TQ, L)
        s = _dot_bf16(qc_ref[0] * (D_HEAD ** -0.5), kv_scr[:kvw, :]) + bias
        p = jnp.exp(s - jnp.max(s, axis=-1, keepdims=True))
        o = _dot_bf16(p, kv_scr[kvw:, :], _NT) / jnp.sum(p, axis=-1, keepdims=True)
        grp = n_heads // C_KV_HEADS
        o_ref[0] = jnp.concatenate(
            [o[h * TQ:(h + 1) * TQ, (h // grp) * D_HEAD:(h // grp + 1) * D_HEAD] for h in range(n_heads)], axis=1)


def dsa_decode(qc, qi, wi, pages, layer, table, new_rows, pos0):
    B, T, W = qc.shape
    n_heads = W // D_HEAD
    n_pg = table.shape[1]
    assert n_pg % 2 == 0 and T <= SUBLANES and pos0 == n_pg * PAGE_SIZE
    n_steps = n_pg // 2
    last = n_steps - 1
    TQ = SUBLANES
    L = n_steps * MOBA_BLOCK + LANES
    kvw = C_KV_HEADS * D_HEAD
    grp = n_heads // C_KV_HEADS
    padq = lambda a: jnp.pad(a, ((0, 0), (0, TQ - T), (0, 0)))
    qi_rows = padq(qi).reshape(B, TQ * IDX_HEADS, IDX_DIM)
    wi_rows = jnp.broadcast_to((padq(wi) * IDX_HEADS ** -0.5).reshape(B, TQ * IDX_HEADS, 1),
                               (B, TQ * IDX_HEADS, LANES))
    qh = padq(qc).reshape(B, TQ, n_heads, D_HEAD).transpose(0, 2, 1, 3)
    z = jnp.zeros_like(qh)
    first = (jnp.arange(n_heads) < grp)[None, :, None, None]
    qc_rows = jnp.concatenate([jnp.where(first, qh, z), jnp.where(first, z, qh)], axis=-1)
    qc_rows = qc_rows.reshape(B, n_heads * TQ, kvw)
    pages = jnp.swapaxes(pages, 2, 3)
    tail = jnp.pad(jnp.swapaxes(new_rows, 1, 2), ((0, 0), (0, 0), (0, PAGE_SIZE - T)))

    def page_map(k):
        return lambda b, j, tbl: (layer, tbl[b, 2 * jnp.minimum(j, last) + k], 0, 0)

    R = pages.shape[2]
    fixed = lambda r, c: pl.BlockSpec((1, r, c), lambda b, j, tbl: (b, 0, 0))
    out = pl.pallas_call(
        functools.partial(_dsa_decode_kernel, n_steps=n_steps, t_new=T, topk=min(DSA_TOPK, (pos0 + T) // 4),
                          n_heads=n_heads, pos0=pos0),
        grid_spec=pltpu.PrefetchScalarGridSpec(
            num_scalar_prefetch=1,
            grid=(B, n_steps + 1),
            in_specs=[pl.BlockSpec((1, 1, R, PAGE_SIZE), page_map(0)),
                      pl.BlockSpec((1, 1, R, PAGE_SIZE), page_map(1)),
                      fixed(R, PAGE_SIZE), fixed(TQ * IDX_HEADS, IDX_DIM), fixed(TQ * IDX_HEADS, LANES),
                      fixed(n_heads * TQ, kvw)],
            out_specs=fixed(TQ, W),
            scratch_shapes=[pltpu.VMEM((2 * kvw, L), BF16), pltpu.VMEM((TQ, L), F32),
                            pltpu.VMEM((TQ, L), jnp.int32), pltpu.VMEM((TQ, L), F32)]),
        out_shape=jax.ShapeDtypeStruct((B, TQ, W), F32),
        compiler_params=_params("parallel", "arbitrary"),
        name="dsa_decode",
    )(table, pages, pages, tail, qi_rows, wi_rows, qc_rows)
    return out[:, :T]


def dsa_attention(q3, drow, kv, pos0, n_keys):
    B, T, _ = q3.shape
    L, R = kv.shape[1], kv.shape[2]
    W = q3.shape[2] // 3
    tq = _row_tile(T, 128)
    topk = min(DSA_TOPK, n_keys // 4)
    return pl.pallas_call(
        functools.partial(_dsa_kernel, pos0=pos0, tq=tq, topk=topk, n_heads=W // D_HEAD),
        grid=(B, T // tq),
        in_specs=[pl.BlockSpec((1, tq, W), lambda b, i: (b, i, 1)),
                  pl.BlockSpec((1, tq, W), lambda b, i: (b, i, 2)),
                  pl.BlockSpec((1, tq, drow.shape[2]), lambda b, i: (b, i, 0)),
                  pl.BlockSpec((1, L, R), lambda b, i: (b, 0, 0))],
        out_specs=pl.BlockSpec((1, tq, W), lambda b, i: (b, i, 0)),
        out_shape=jax.ShapeDtypeStruct((B, T, W), F32),
        scratch_shapes=[pltpu.VMEM((tq, L), jnp.int32), pltpu.VMEM((tq, L), F32)],
        compiler_params=_params("parallel", "parallel"),
        name="dsa_attention",
    )(q3, q3, drow, kv)


def _segsum(x, ones_blk):
    hi = x.astype(BF16)
    r1 = x - hi.astype(F32)
    mid = r1.astype(BF16)
    lo = (r1 - mid.astype(F32)).astype(BF16)
    d = lambda a: jnp.dot(a, ones_blk, preferred_element_type=F32)
    return d(hi) + (d(mid) + d(lo))


def _rwkv_prep_kernel(z_ref, zs_ref, mu_ref, lora_ref, vec_ref, ones_ref, rs_ref, gb_ref, *t3_ref, width):
    W = width
    z = z_ref[0]
    xm = z + (zs_ref[0] - z) * mu_ref[...]
    r, k, v, g = (xm[:, n * W:(n + 1) * W] for n in range(4))
    lat = xm[:, 4 * W:4 * W + 2 * B_LORA]
    lane = lax.broadcasted_iota(jnp.int32, (1, 2 * B_LORA), 1)
    lat = jnp.where(lane < B_LORA, jnp.tanh(lat), lat)
    up = _dot3(lat, lora_ref[...], _NN)
    w0, a0, k_k, k_a, r_k = (vec_ref[n:n + 1, :] for n in range(5))
    wpre = -(w0 + up[:, :W])
    softplus = jnp.maximum(wpre, 0.0) + jnp.log(1.0 + jnp.exp(-jnp.abs(wpre)))
    log_decay = -jnp.exp(-softplus - 0.5)
    a = jax.nn.sigmoid(a0 + up[:, W:])
    ones_blk = ones_ref[...]
    kk = k * k_k
    kk = kk / jnp.maximum(jnp.sqrt(_segsum(kk * kk, ones_blk)), 1e-12)
    k = k * (1.0 + (a - 1.0) * k_a)
    kka = kk * a
    bonus = _segsum(r * k * r_k, ones_blk) * v
    rs_ref[0] = jnp.concatenate([r, jnp.exp(log_decay), k, kk, kka, v, log_decay], axis=1)
    gb_ref[0] = jnp.concatenate([jax.nn.sigmoid(g), bonus], axis=1)
    if t3_ref:
        t3_ref[0][0] = jnp.concatenate([log_decay.T, kka.T, k.T], axis=0)


RS_SEGMENTS = 7


def rwkv_prep(zr, zs, mu, lora, vecs, ones_blk, key_major):
    B, T, C = zr.shape
    W = vecs.shape[1]
    tm = _row_tile(T, 256)
    full = lambda a: pl.BlockSpec(a.shape, lambda b, i: (0,) * a.ndim)
    row = lambda c: pl.BlockSpec((1, tm, c), lambda b, i: (b, i, 0))
    out_specs = [row(RS_SEGMENTS * W), row(2 * W)]
    out_shape = [jax.ShapeDtypeStruct((B, T, RS_SEGMENTS * W), F32), jax.ShapeDtypeStruct((B, T, 2 * W), F32)]
    if key_major:
        out_specs.append(pl.BlockSpec((1, 3 * W, tm), lambda b, i: (b, 0, i)))
        out_shape.append(jax.ShapeDtypeStruct((B, 3 * W, T), F32))
    return pl.pallas_call(
        functools.partial(_rwkv_prep_kernel, width=W),
        grid=(B, T // tm),
        in_specs=[row(C), row(C), full(mu), full(lora), full(vecs), full(ones_blk)],
        out_specs=out_specs,
        out_shape=out_shape,
        compiler_params=_params("parallel", "parallel"),
        name="rwkv_prep",
    )(zr, zs, mu, lora, vecs, ones_blk)


RWKV_CHUNK = 64


def _split(a):
    return _split_bf16(a)


def _dot3s(a, b, dims=_NN):
    dn = (dims, ((), ()))
    d = lambda x, y: lax.dot_general(x, y, dn, preferred_element_type=F32)
    return d(a[0], b[0]) + (d(a[0], b[1]) + d(a[1], b[0]))


def _rows(s, lo, hi):
    return s[0][lo:hi], s[1][lo:hi]


def _rwkv_chunk_kernel(rs_ref, t3_ref, gb_ref, s0_ref, ln_ref, y_ref, sn_ref, st_scr, *, bb, width):
    c = pl.program_id(1)
    W = width
    n_hp = W // LANES
    C = RWKV_CHUNK
    TL = 2 * C
    assert TL == LANES and C == D_HEAD

    @pl.when(c == 0)
    def _():
        st_scr[...] = s0_ref[...]

    ri = lax.broadcasted_iota(jnp.int32, (TL, TL), 0)
    ci = lax.broadcasted_iota(jnp.int32, (TL, TL), 1)
    same = (ri // C) == (ci // C)
    strict = same & (ci < ri)
    incl = same & (ci <= ri)
    low_blk = jnp.where(incl, 1.0, 0.0).astype(BF16)
    up_blk = jnp.where(same & (ri <= ci), 1.0, 0.0).astype(BF16)
    ones_blk = jnp.where(same, 1.0, 0.0).astype(BF16)
    eye = jnp.where(ri == ci, 1.0, 0.0)
    lane = lax.broadcasted_iota(jnp.int32, (1, TL), 1)
    left = lane < C
    zeros_c = jnp.zeros((C, TL), F32)

    def exact3(x, w, x_first):
        hi = x.astype(BF16)
        r1 = x - hi.astype(F32)
        mid = r1.astype(BF16)
        lo = (r1 - mid.astype(F32)).astype(BF16)
        d = (lambda a: jnp.dot(a, w, preferred_element_type=F32)) if x_first else \
            (lambda a: jnp.dot(w, a, preferred_element_type=F32))
        return d(hi) + (d(mid) + d(lo))

    pairs = []
    for p in range(bb * n_hp):
        b, hp = divmod(p, n_hp)
        seg = lambda n: rs_ref[b, :, pl.ds(n * W + hp * LANES, LANES)]
        r, k, kk, kka, v, lw = seg(0), seg(2), seg(3), seg(4), seg(5), seg(6)
        lw_t, kka_t, k_t = (t3_ref[b, pl.ds(n * W + hp * LANES, LANES), :] for n in range(3))
        cum = exact3(lw, low_blk, False)
        cum_t = exact3(lw_t, up_blk, True)
        a_til = -kk * jnp.exp(cum - lw)
        r_til = r * jnp.exp(cum)
        inv = jnp.exp(-cum)
        bk = _split(jnp.concatenate([kka * inv, k * inv], axis=0))
        cend = jnp.where(left, cum_t[:, C - 1:C], cum_t[:, TL - 1:TL])
        rest = jnp.exp(cend - cum_t)
        d = dict(b=b, hp=hp, v=v, vs=_split(v), a_til=a_til, r_til=r_til, cum_t=cum_t,
                 bk_hat_t=jnp.concatenate([kka_t * rest, k_t * rest], axis=1),
                 lak=[], mrbk=[], x=[], t=[])
        for h in range(2):
            mh = (lane // C) == h
            ar = _split(jnp.concatenate([jnp.where(mh, a_til, 0.0), jnp.where(mh, r_til, 0.0)], axis=0))
            m = _dot3s(ar, bk, _NT)
            lab = jnp.where(strict, m[:TL, :TL], 0.0)
            d["lak"].append(_split(jnp.where(strict, m[:TL, TL:], 0.0)))
            d["mrbk"].append(_split(jnp.concatenate([jnp.where(incl, m[TL:, :TL], 0.0),
                                                     jnp.where(incl, m[TL:, TL:], 0.0)], axis=1)))
            d["x"].append(_split(lab))
            d["t"].append(eye + lab)
        pairs.append(d)

    for _ in range(5):
        for d in pairs:
            for h in range(2):
                d["x"][h] = _split(_dot3s(d["x"][h], d["x"][h]))
                d["t"][h] = d["t"][h] + _dot3s(d["x"][h], _split(d["t"][h]))
    for p, d in enumerate(pairs):
        d["tinv"] = [_split(t) for t in d["t"]]
        d["st"] = st_scr[p]
        d["ys"] = []
        d["u_ext"] = None

    for s in range(2):
        lo, hi = s * C, (s + 1) * C
        in_chunk = (lax.broadcasted_iota(jnp.int32, (1, 2 * TL), 1) % TL) // C == s
        for d in pairs:
            lak, tinv, mrbk, vs = d["lak"], d["tinv"], d["mrbk"], d["vs"]
            sts = _split(d["st"])
            ars = _dot3s(_split(jnp.concatenate([d["a_til"][lo:hi], d["r_til"][lo:hi]], axis=0)), sts)
            rhs = ars[:C] + jnp.where(left, _dot3s(_rows(lak[0], lo, hi), vs), _dot3s(_rows(lak[1], lo, hi), vs))
            rhs_ext = _split(jnp.concatenate([rhs, zeros_c] if s == 0 else [zeros_c, rhs], axis=0))
            u = jnp.where(left, _dot3s(_rows(tinv[0], lo, hi), rhs_ext), _dot3s(_rows(tinv[1], lo, hi), rhs_ext))
            d["u_ext"] = jnp.concatenate([u, zeros_c] if s == 0 else [d["u_ext"][:C], u], axis=0)
            uv = _split(jnp.concatenate([d["u_ext"], d["v"]], axis=0))
            d["ys"].append(ars[C:] + jnp.where(left, _dot3s(_rows(mrbk[0], lo, hi), uv),
                                               _dot3s(_rows(mrbk[1], lo, hi), uv)))
            upd = _dot3s(_split(jnp.where(in_chunk, d["bk_hat_t"], 0.0)), uv)
            d["st"] = jnp.where(same, jnp.exp(d["cum_t"][:, hi - 1:hi]) * d["st"] + upd, 0.0)

    for p, d in enumerate(pairs):
        st_scr[p] = d["st"]
        b, hp = d["b"], d["hp"]
        y = jnp.concatenate(d["ys"], axis=0)
        mean = exact3(y, ones_blk, True) * (1.0 / C)
        var = exact3(jnp.square(y - mean), ones_blk, True) * (1.0 / C)
        cols = pl.ds(hp * LANES, LANES)
        yn = (y - mean) * lax.rsqrt(var + RWKV_GN_EPS) * ln_ref[0:1, cols] + ln_ref[1:2, cols]
        y_ref[b, :, cols] = (yn + gb_ref[b, :, pl.ds(W + hp * LANES, LANES)]) * gb_ref[b, :, cols]

    @pl.when(c == pl.num_programs(1) - 1)
    def _():
        sn_ref[...] = st_scr[...]


def rwkv_chunk_scan(rs, t3, gb, s0, ln_w, ln_b):
    B, T, _ = rs.shape
    W = gb.shape[2] // 2
    n_hp = W // LANES
    TL = 2 * RWKV_CHUNK
    bb = 2 if B % 2 == 0 else 1
    assert T % TL == 0
    ln = jnp.concatenate([ln_w.reshape(1, W), ln_b.reshape(1, W), jnp.zeros((SUBLANES - 2, W), F32)], axis=0)
    st_spec = pl.BlockSpec((bb * n_hp, LANES, LANES), lambda g, c: (g, 0, 0))
    return pl.pallas_call(
        functools.partial(_rwkv_chunk_kernel, bb=bb, width=W),
        grid=(B // bb, T // TL),
        in_specs=[pl.BlockSpec((bb, TL, RS_SEGMENTS * W), lambda g, c: (g, c, 0)),
                  pl.BlockSpec((bb, 3 * W, TL), lambda g, c: (g, 0, c)),
                  pl.BlockSpec((bb, TL, 2 * W), lambda g, c: (g, c, 0)),
                  st_spec,
                  pl.BlockSpec((SUBLANES, W), lambda g, c: (0, 0))],
        out_specs=[pl.BlockSpec((bb, TL, W), lambda g, c: (g, c, 0)), st_spec],
        out_shape=[jax.ShapeDtypeStruct((B, T, W), F32),
                   jax.ShapeDtypeStruct((B * n_hp, LANES, LANES), F32)],
        scratch_shapes=[pltpu.VMEM((bb * n_hp, LANES, LANES), F32)],
        compiler_params=_params("parallel", "arbitrary"),
        name="rwkv_chunk_scan",
    )(rs, t3, gb, s0, ln)


def _rwkv_scan_kernel(rs_ref, gb_ref, s0_ref, lnw_ref, lnb_ref, y_ref, sn_ref, s_scr, yt_scr,
                      *, bb, tc, t_total, width):
    c = pl.program_id(1)
    n_hp = width // LANES
    n_pair = bb * n_hp

    @pl.when(c == 0)
    def _():
        s_scr[...] = s0_ref[...]

    yt_scr[...] = jnp.zeros(yt_scr.shape, F32)
    lane = lax.broadcasted_iota(jnp.int32, (1, LANES), 1)
    left = lane < D_HEAD
    eye2 = jnp.where(lax.broadcasted_iota(jnp.int32, (D_HEAD, LANES), 0)
                     == lax.broadcasted_iota(jnp.int32, (D_HEAD, LANES), 1) % D_HEAD, 1.0, 0.0)
    tok = lax.broadcasted_iota(jnp.int32, (1, tc), 1)

    def halves(p):
        return (jnp.sum(jnp.where(left, p, 0.0), axis=-1, keepdims=True),
                jnp.sum(jnp.where(left, 0.0, p), axis=-1, keepdims=True))

    def step(g, carry):
        t0 = pl.multiple_of(g * SUBLANES, SUBLANES)
        for p in range(n_pair):
            b, hp = divmod(p, n_hp)
            grp = [rs_ref[b, pl.ds(t0, SUBLANES), pl.ds(n * width + hp * LANES, LANES)] for n in range(6)]
            s = s_scr[p]
            y_l, y_r = yt_scr[2 * p], yt_scr[2 * p + 1]
            for j in range(SUBLANES):
                r, w, k, kk, kka, v = (a[j:j + 1] for a in grp)
                sa_l, sa_r = halves(s * kk)
                v_l, v_r = halves(eye2 * v)
                s = s * w - jnp.where(left, sa_l, sa_r) * kka + jnp.where(left, v_l, v_r) * k
                o_l, o_r = halves(s * r)
                here = tok == t0 + j
                y_l = jnp.where(here, o_l, y_l)
                y_r = jnp.where(here, o_r, y_r)
            s_scr[p] = s
            yt_scr[2 * p], yt_scr[2 * p + 1] = y_l, y_r
        return carry

    n_tok = jnp.minimum(tc, t_total - c * tc)
    lax.fori_loop(0, (n_tok + SUBLANES - 1) // SUBLANES, step, 0)

    for p in range(n_pair):
        b, hp = divmod(p, n_hp)
        rows = []
        for h2 in range(2):
            y = yt_scr[2 * p + h2]
            mean = jnp.mean(y, axis=0, keepdims=True)
            var = jnp.mean(jnp.square(y - mean), axis=0, keepdims=True)
            seg = pl.ds(hp * LANES + h2 * D_HEAD, D_HEAD)
            rows.append((y - mean) * lax.rsqrt(var + RWKV_GN_EPS) * lnw_ref[seg, :] + lnb_ref[seg, :])
        yn = jnp.concatenate(rows, axis=0).T
        cols = pl.ds(hp * LANES, LANES)
        y_ref[b, :, cols] = (yn + gb_ref[b, :, pl.ds(width + hp * LANES, LANES)]) * gb_ref[b, :, cols]

    @pl.when(c == pl.num_programs(1) - 1)
    def _():
        sn_ref[...] = s_scr[...]


def rwkv_scan(rs, gb, s0, ln_w, ln_b, t_total):
    B, Tp, _ = rs.shape
    W = gb.shape[2] // 2
    n_hp = W // LANES
    tc = LANES
    bb = min(B, 4)
    assert B % bb == 0 and Tp % tc == 0
    lnw = jnp.broadcast_to(ln_w.reshape(W, 1), (W, tc))
    lnb = jnp.broadcast_to(ln_b.reshape(W, 1), (W, tc))
    return pl.pallas_call(
        functools.partial(_rwkv_scan_kernel, bb=bb, tc=tc, t_total=t_total, width=W),
        grid=(B // bb, Tp // tc),
        in_specs=[pl.BlockSpec((bb, tc, RS_SEGMENTS * W), lambda g, c: (g, c, 0)),
                  pl.BlockSpec((bb, tc, 2 * W), lambda g, c: (g, c, 0)),
                  pl.BlockSpec((bb * n_hp, D_HEAD, LANES), lambda g, c: (g, 0, 0)),
                  pl.BlockSpec((W, tc), lambda g, c: (0, 0)),
                  pl.BlockSpec((W, tc), lambda g, c: (0, 0))],
        out_specs=[pl.BlockSpec((bb, tc, W), lambda g, c: (g, c, 0)),
                   pl.BlockSpec((bb * n_hp, D_HEAD, LANES), lambda g, c: (g, 0, 0))],
        out_shape=[jax.ShapeDtypeStruct((B, Tp, W), F32),
                   jax.ShapeDtypeStruct((B * n_hp, D_HEAD, LANES), F32)],
        scratch_shapes=[pltpu.VMEM((bb * n_hp, D_HEAD, LANES), F32),
                        pltpu.VMEM((2 * bb * n_hp, D_HEAD, tc), F32)],
        compiler_params=_params("parallel", "arbitrary"),
        name="rwkv_scan",
    )(rs, gb, s0, lnw, lnb)


def _merge_kernel(ya_ref, yb_ref, yc_ref, gl_ref, x_ref, gt_ref, wb_ref, wo_ref, o_ref):
    D = x_ref.shape[2]
    mix = None
    for n, y_ref in enumerate((ya_ref, yb_ref, yc_ref)):
        proj = _dot_bf16(y_ref[0], wb_ref[n])
        gate = jax.nn.sigmoid(gl_ref[0, :, n * D:(n + 1) * D].astype(F32))
        mix = gate * proj if mix is None else mix + gate * proj
    o_ref[0] = x_ref[0] + gt_ref[0] * _dot_bf16(mix, wo_ref[...])


def merge(ya, yb, yc, gl, x, gt, wb, wo):
    B, T, D = x.shape
    W = ya.shape[2]
    tm = _row_tile(T, 256)
    tmod = tm if gt.shape[1] == T else 1
    mod_map = (lambda b, i: (b, i, 0)) if gt.shape[1] == T else (lambda b, i: (b, 0, 0))
    row = lambda c: pl.BlockSpec((1, tm, c), lambda b, i: (b, i, 0))
    return pl.pallas_call(
        _merge_kernel,
        grid=(B, T // tm),
        in_specs=[row(W), row(W), row(W), row(N_BRANCH * D), row(D), pl.BlockSpec((1, tmod, D), mod_map),
                  pl.BlockSpec(wb.shape, lambda b, i: (0, 0, 0)), pl.BlockSpec(wo.shape, lambda b, i: (0, 0))],
        out_specs=row(D),
        out_shape=jax.ShapeDtypeStruct((B, T, D), F32),
        compiler_params=_params("parallel", "parallel"),
        name="merge",
    )(ya, yb, yc, gl, x, gt, wb, wo)


def _top_values(x, n):
    vals = []
    for _ in range(n):
        m = jnp.max(x, axis=0, keepdims=True)
        vals.append(m)
        x = jnp.where(x == m, -jnp.inf, x)
    return vals


def _peer_route_kernel(x_ref, sc_ref, sh_ref, g_ref, wq_ref, keys_ref, ht_ref, s1_ref, s2_ref, e2_ref, st_ref):
    x = x_ref[0]
    h = x * lax.rsqrt(jnp.mean(x * x, axis=-1, keepdims=True) + EPS) * g_ref[...]
    h = h * (1.0 + sc_ref[0]) + sh_ref[0]
    ht_ref[...] = h.T.astype(BF16)
    q = _dot_bf16(h, wq_ref[...])
    tm = x.shape[0]
    for hd in range(PEER_HEADS):
        s1, s2 = (_dot_bf16(keys_ref[2 * hd + p], q[:, (2 * hd + p) * LANES:(2 * hd + p + 1) * LANES], _NT)
                  for p in range(2))
        t1 = _top_values(s1, PEER_TOPK)
        t2 = jnp.concatenate(_top_values(s2, PEER_TOPK), axis=0)
        half = PEER_TOPK // 2
        row = lax.broadcasted_iota(jnp.int32, (half, 1), 0)
        cand = [t1[0] + t2, t1[1] + t2[:half]]
        cand += [jnp.where(row < PEER_TOPK // (a + 1), t1[a] + t2[:half], -jnp.inf) for a in range(2, half)]
        cand.append(jnp.concatenate(t1[half:], axis=0) + t2[0:1])
        best = _top_values(jnp.concatenate(cand, axis=0), PEER_TOPK)
        z = sum(jnp.exp(b - best[0]) for b in best)
        s1_ref[hd] = s1
        s2_ref[hd] = s2
        e2_ref[hd] = jnp.exp(s2 - t2[0:1])
        st_ref[hd] = jnp.concatenate([best[-1], t2[0:1] - best[0], 1.0 / z, jnp.zeros((SUBLANES - 3, tm), F32)],
                                     axis=0)


def peer_route(x, scale, shift, g, wq, keys):
    B, T, D = x.shape
    N = B * T
    tm = _row_tile(T, 256)
    nt = T // tm
    tmod = tm if scale.shape[1] == T else 1
    mod_map = (lambda b, i: (b, i, 0)) if scale.shape[1] == T else (lambda b, i: (b, 0, 0))
    hkn = jax.ShapeDtypeStruct((PEER_HEADS, PEER_KEYS, N), F32)
    hk_spec = pl.BlockSpec((PEER_HEADS, PEER_KEYS, tm), lambda b, i: (0, 0, b * nt + i))
    return pl.pallas_call(
        _peer_route_kernel,
        grid=(B, nt),
        in_specs=[pl.BlockSpec((1, tm, D), lambda b, i: (b, i, 0)),
                  pl.BlockSpec((1, tmod, D), mod_map), pl.BlockSpec((1, tmod, D), mod_map),
                  pl.BlockSpec((1, D), lambda b, i: (0, 0)),
                  pl.BlockSpec(wq.shape, lambda b, i: (0, 0)),
                  pl.BlockSpec(keys.shape, lambda b, i: (0, 0, 0))],
        out_specs=[pl.BlockSpec((D, tm), lambda b, i: (0, b * nt + i)), hk_spec, hk_spec, hk_spec,
                   pl.BlockSpec((PEER_HEADS, SUBLANES, tm), lambda b, i: (0, 0, b * nt + i))],
        out_shape=[jax.ShapeDtypeStruct((D, N), BF16), hkn, hkn, hkn,
                   jax.ShapeDtypeStruct((PEER_HEADS, SUBLANES, N), F32)],
        compiler_params=_params("parallel", "parallel"),
        name="peer_route",
    )(x, scale, shift, g.reshape(1, D), wq, keys)


def _peer_expert_kernel(ht_ref, s1_ref, s2_ref, e2_ref, st_ref, u_ref, vt_ref, x_ref, gt_ref, o_ref, acc_ref):
    e = pl.program_id(2)

    @pl.when(e == 0)
    def _():
        acc_ref[...] = jnp.zeros(acc_ref.shape, F32)

    ht = ht_ref[...]
    upd = None
    group = 2 * PEER_KEYS
    for gi in range(u_ref.shape[0] // group):
        act = jnp.dot(u_ref[gi * group:(gi + 1) * group, :], ht, preferred_element_type=F32)
        ws = []
        for r in range(2 * gi, 2 * gi + 2):
            gate = None
            for hd in range(PEER_HEADS):
                s1 = s1_ref[hd, r:r + 1, :]
                e1 = jnp.exp(s1 + st_ref[hd, 1:2, :]) * st_ref[hd, 2:3, :]
                gh = jnp.where(s2_ref[hd] + s1 >= st_ref[hd, 0:1, :], e2_ref[hd] * e1, 0.0)
                gate = gh if gate is None else gate + gh
            a = act[(r - 2 * gi) * PEER_KEYS:(r - 2 * gi + 1) * PEER_KEYS]
            gelu = 0.5 * a * (1.0 + lax.erf(a * (2.0 ** -0.5)))
            ws.append((gate * gelu).astype(BF16))
        d = jnp.dot(vt_ref[:, gi * group:(gi + 1) * group], jnp.concatenate(ws, axis=0),
                    preferred_element_type=F32)
        upd = d if upd is None else upd + d
    acc_ref[...] += upd

    @pl.when(e == pl.num_programs(2) - 1)
    def _():
        o_ref[0] = x_ref[0] + gt_ref[0] * acc_ref[...].T


def peer_expert(ht, s1, s2, e2, st, u, vt, x, gt):
    B, T, D = x.shape
    E = u.shape[0]
    tm = _row_tile(T, 256)
    nt = T // tm
    te = SUBLANES * PEER_KEYS
    tmod = tm if gt.shape[1] == T else 1
    mod_map = (lambda b, i, e: (b, i, 0)) if gt.shape[1] == T else (lambda b, i, e: (b, 0, 0))
    hk_spec = pl.BlockSpec((PEER_HEADS, PEER_KEYS, tm), lambda b, i, e: (0, 0, b * nt + i))
    return pl.pallas_call(
        _peer_expert_kernel,
        grid=(B, nt, E // te),
        in_specs=[pl.BlockSpec((D, tm), lambda b, i, e: (0, b * nt + i)),
                  pl.BlockSpec((PEER_HEADS, SUBLANES, tm), lambda b, i, e: (0, e, b * nt + i)),
                  hk_spec, hk_spec,
                  pl.BlockSpec((PEER_HEADS, SUBLANES, tm), lambda b, i, e: (0, 0, b * nt + i)),
                  pl.BlockSpec((te, D), lambda b, i, e: (e, 0)),
                  pl.BlockSpec((D, te), lambda b, i, e: (0, e)),
                  pl.BlockSpec((1, tm, D), lambda b, i, e: (b, i, 0)),
                  pl.BlockSpec((1, tmod, D), mod_map)],
        out_specs=pl.BlockSpec((1, tm, D), lambda b, i, e: (b, i, 0)),
        out_shape=jax.ShapeDtypeStruct((B, T, D), F32),
        scratch_shapes=[pltpu.VMEM((D, tm), F32)],
        compiler_params=_params("parallel", "parallel", "arbitrary"),
        name="peer_expert",
    )(ht, s1, s2, e2, st, u, vt, x, gt)


def _pack_state(s):
    B, H, dv, dk = s.shape
    return s.reshape(B, H // 2, 2, dv, dk).transpose(0, 1, 3, 2, 4).reshape(B * H // 2, dv, 2 * dk)


def _unpack_state(s, B):
    P, dv, dk2 = s.shape
    H = 2 * P // B
    return s.reshape(B, H // 2, dv, 2, dk2 // 2).transpose(0, 1, 3, 2, 4).reshape(B, H, dv, dk2 // 2)


def _pack_state_bd(s):
    B, H, dv, dk = s.shape
    st = s.transpose(0, 1, 3, 2).reshape(B, H // 2, 2, dk, dv)
    z = jnp.zeros((B, H // 2, dk, dv), s.dtype)
    top = jnp.concatenate([st[:, :, 0], z], axis=-1)
    bot = jnp.concatenate([z, st[:, :, 1]], axis=-1)
    return jnp.concatenate([top, bot], axis=-2).reshape(B * H // 2, 2 * dk, 2 * dv)


def _unpack_state_bd(s, B):
    P, dk2, dv2 = s.shape
    s = s.reshape(B, P // B, 2, dk2 // 2, 2, dv2 // 2)
    d = jnp.stack([s[:, :, 0, :, 0, :], s[:, :, 1, :, 1, :]], axis=2)
    return d.reshape(B, 2 * P // B, dk2 // 2, dv2 // 2).transpose(0, 1, 3, 2)


def _block(x, ada, pos0, past, shift_prev, state, lw, rowwise_mods):
    B, T, D = x.shape
    N = B * T
    sh1, sc1, gt1, sh2, sc2, gt2 = ada
    if rowwise_mods:
        xr = x.reshape(1, N, D)
        mod = lambda m: jnp.repeat(m, T, axis=0).reshape(1, N, D)
    else:
        xr = x
        mod = lambda m: m.reshape(B, 1, D)
    h = modulate(xr, mod(sc1), mod(sh1), lw["norm_mix"], BF16).reshape(N, D)
    q3 = matmul(h, lw["w_q3"])
    mrow = matmul(h, lw["w_moba"])
    zr = matmul(h, lw["w_rw"], tn_want=lw["w_rw"].shape[1])
    drow = matmul(h, lw["w_dsa"])
    gl = matmul(h, lw["w_gl"], out_dtype=BF16)
    MR = mrow.shape[1]
    RW = zr.shape[1]
    q3b = q3.reshape(B, T, -1)
    drow_b = drow.reshape(B, T, -1)

    if past is None:
        ident = jnp.arange(N // PAGE_SIZE, dtype=jnp.int32).reshape(B, T // PAGE_SIZE)
        kv_a, km = paged_cast(mrow.reshape(1, N // PAGE_SIZE, PAGE_SIZE, MR), 0, ident, None, MR // 2)
        kv_c = paged_cast(drow.reshape(1, N // PAGE_SIZE, PAGE_SIZE, drow.shape[1]), 0, ident, None)
        nbp = -(-km.shape[1] // 16) * 16
        km = jnp.pad(km, ((0, 0), (0, nbp - km.shape[1]), (0, 0)))
        ya = moba_attention(q3b, kv_a, km, pos0)
        yc = dsa_attention(q3b, drow_b, kv_c, pos0, pos0 + T)
    else:
        pages_a, pages_c, layer, table = past
        A = MR // 2
        ya = moba_decode(q3b[:, :, :A], pages_a, layer, table, mrow.reshape(B, T, MR))
        yc = dsa_decode(q3b[:, :, A:2 * A], q3b[:, :, 2 * A:], drow_b[:, :, DSA_ROW:DSA_ROW + IDX_HEADS],
                        pages_c, layer, table, drow_b[:, :, :DSA_ROW], pos0)

    zr_b = zr.reshape(B, T, RW)
    zs = jnp.concatenate([shift_prev[:, None], zr_b[:, :-1]], axis=1)
    prep_w = (lw["rwkv_mu"], lw["rwkv_lora"], lw["rwkv_vecs"], lw["ones_blk"])
    if T % (2 * RWKV_CHUNK) == 0:
        rs, gb, t3 = rwkv_prep(zr_b, zs, *prep_w, True)
        yb, s_new = rwkv_chunk_scan(rs, t3, gb, _pack_state_bd(state), lw["rwkv_ln_w"], lw["rwkv_ln_b"])
        s_new = _unpack_state_bd(s_new, B)
    else:
        rs, gb = rwkv_prep(zr_b.reshape(1, N, RW), zs.reshape(1, N, RW), *prep_w, False)
        Tp = -(-T // LANES) * LANES
        padt = lambda a: jnp.pad(a.reshape(B, T, -1), ((0, 0), (0, Tp - T), (0, 0)))
        A = gb.shape[2] // 2
        rs_p = padt(rs).at[:, T:, A:2 * A].set(1.0)
        yb, s_new = rwkv_scan(rs_p, padt(gb), _pack_state(state), lw["rwkv_ln_w"], lw["rwkv_ln_b"], T)
        yb = yb[:, :T]
        s_new = _unpack_state(s_new, B)

    W = ya.shape[2]
    rsh = (lambda a: a.reshape(1, N, -1)) if rowwise_mods else (lambda a: a.reshape(B, T, -1))
    x1 = merge(rsh(ya), rsh(yb), rsh(yc), rsh(gl), xr, mod(gt1), lw["w_branch"], lw["w_out"])
    ht, s1, s2, e2, st = peer_route(x1, mod(sc2), mod(sh2), lw["norm_ffn"], lw["peer_wq"], lw["peer_keys"])
    x2 = peer_expert(ht, s1, s2, e2, st, lw["peer_u"], lw["peer_vt"], x1, mod(gt2))
    return (x2.reshape(B, T, D), mrow.reshape(B, T, MR), drow_b[:, :, :DSA_ROW],
            s_new, zr_b[:, -1])


def kernel(x_prompt, x_sample, cache_moba, cache_dsa, state_rwkv, state_shift, page_table, c_prompt, c_sample, w_ada, b_ada, norm_mix, w_in, rwkv_mu, rwkv_w0, rwkv_w_up, rwkv_a0, rwkv_a_up, rwkv_k_k, rwkv_k_a, rwkv_r_k, rwkv_ln_w, rwkv_ln_b, w_branch, w_out, norm_ffn, peer_wq, peer_keys, peer_u, peer_v, norm_final):
    n_b, n_t, D = x_prompt.shape
    n_db, n_dt, _ = x_sample.shape
    depth = w_in.shape[0]
    past_len = page_table.shape[1] * PAGE_SIZE
    A = D // 2
    RW = 4 * A + 2 * B_LORA
    idx_w = IDX_HEADS * IDX_DIM
    sizes = (A, 2 * A, RW, A, idx_w, IDX_HEADS, DSA_ROW, N_BRANCH * D)
    off = np.concatenate([[0], np.cumsum(sizes)])
    assert off[-1] == w_in.shape[2]

    n_c = n_b + n_db
    n_cp = -(-n_c // SUBLANES) * SUBLANES
    c_all = jnp.pad(jnp.concatenate([c_prompt, c_sample], axis=0), ((0, n_cp - n_c), (0, 0)))
    ada_all = ada_matmul(c_all, w_ada, b_ada)

    ones_blk = jnp.asarray(np.kron(np.eye(A // D_HEAD), np.ones((D_HEAD, D_HEAD))), BF16)
    xp, xs = x_prompt, x_sample
    outs = [[] for _ in range(8)]
    for l in range(depth):
        wl = w_in[l]
        seg = lambda n: wl[:, off[n]:off[n + 1]]
        dsa_pad = LANES * (-(-(DSA_ROW + IDX_HEADS) // LANES)) - DSA_ROW - IDX_HEADS
        lora = jnp.zeros((2 * B_LORA, 2 * A), F32)
        lora = lora.at[:B_LORA, :A].set(rwkv_w_up[l]).at[B_LORA:, A:].set(rwkv_a_up[l])
        flat = lambda a: a.reshape(1, -1)
        lw = dict(
            norm_mix=norm_mix[l], norm_ffn=norm_ffn[l],
            w_q3=jnp.concatenate([seg(0), seg(3), seg(4)], axis=1).astype(BF16),
            w_moba=seg(1).astype(BF16), w_rw=seg(2).astype(BF16),
            w_dsa=jnp.pad(jnp.concatenate([seg(6), seg(5)], axis=1), ((0, 0), (0, dsa_pad))).astype(BF16),
            w_gl=seg(7).astype(BF16),
            rwkv_mu=flat(rwkv_mu[l]), rwkv_lora=lora, ones_blk=ones_blk,
            rwkv_vecs=jnp.concatenate([flat(rwkv_w0[l]), flat(rwkv_a0[l]), flat(rwkv_k_k[l]), flat(rwkv_k_a[l]),
                                       flat(rwkv_r_k[l]), jnp.zeros((SUBLANES - 5, A), F32)], axis=0),
            rwkv_ln_w=rwkv_ln_w[l], rwkv_ln_b=rwkv_ln_b[l],
            w_branch=w_branch[l].astype(BF16), w_out=w_out[l].astype(BF16),
            peer_wq=peer_wq[l].astype(BF16),
            peer_keys=peer_keys[l].reshape(2 * PEER_HEADS, PEER_KEYS, -1).astype(BF16),
            peer_u=peer_u[l].astype(BF16), peer_vt=transpose_cast(peer_v[l], BF16),
        )
        ada_p = jnp.split(ada_all[l, :n_b], 6, axis=-1)
        ada_s = jnp.split(ada_all[l, n_b:n_c], 6, axis=-1)
        xp, m_p, d_p, r_p, s_p = _block(
            xp, ada_p, 0, None, jnp.zeros((n_b, RW), F32),
            jnp.zeros((n_b, A // D_HEAD, D_HEAD, D_HEAD), F32), lw, False)
        xs, m_s, d_s, r_s, s_s = _block(
            xs, ada_s, past_len, (cache_moba, cache_dsa, l, page_table), state_shift[l], state_rwkv[l], lw, True)
        for lst, val in zip(outs, (m_p, m_s, d_p, d_s, r_p, r_s, s_p, s_s)):
            lst.append(val)
    y_prompt = rms_norm_rows(xp.reshape(n_b * n_t, D), norm_final).reshape(n_b, n_t, D)
    y_sample = rms_norm_rows(xs.reshape(n_db * n_dt, D), norm_final).reshape(n_db, n_dt, D)
    return (y_prompt, y_sample) + tuple(jnp.stack(o) for o in outs)
```

```python
import functools

import jax
import jax.numpy as jnp
import numpy as np
from jax import lax
from jax.experimental import pallas as pl
from jax.experimental.pallas import tpu as pltpu

D_HEAD = 64
MOBA_BLOCK = 256
MOBA_TOPK = 3
B_LORA = 64
C_KV_HEADS = 2
IDX_HEADS = 8
IDX_DIM = 64
DSA_TOPK = 256
N_BRANCH = 3
PEER_KEYS = 128
PEER_HEADS = 8
PEER_TOPK = 16
PAGE_SIZE = 128
EPS = 1e-6
RWKV_GN_EPS = 64e-5
NEG = -1e30

LANES = 128
SUBLANES = 8
VMEM_LIMIT_BYTES = 56 * 1024 * 1024

F32 = jnp.float32
BF16 = jnp.bfloat16
INT_MIN = -2 ** 31


def _params(*sem):
    return pltpu.CompilerParams(dimension_semantics=sem, vmem_limit_bytes=VMEM_LIMIT_BYTES)


def _split_bf16(a):
    hi = a.astype(BF16)
    lo = (a - hi.astype(F32)).astype(BF16)
    return hi, lo


def _dot3(a, b, dims):
    ah, al = _split_bf16(a)
    bh, bl = _split_bf16(b)
    dn = (dims, ((), ()))
    d = lambda x, y: lax.dot_general(x, y, dn, preferred_element_type=F32)
    return d(ah, bh) + (d(ah, bl) + d(al, bh))


_NN = ((1,), (0,))
_NT = ((1,), (1,))


def _dot_bf16(a, b, dims=_NN):
    return lax.dot_general(a.astype(BF16), b.astype(BF16), (dims, ((), ())), preferred_element_type=F32)


def _row_tile(n, want):
    t = min(n, want)
    while n % t:
        t //= 2
    return t


def _modulate_kernel(x_ref, sc_ref, sh_ref, g_ref, o_ref):
    x = x_ref[0]
    y = x * lax.rsqrt(jnp.mean(x * x, axis=-1, keepdims=True) + EPS) * g_ref[...]
    o_ref[0] = (y * (1.0 + sc_ref[0]) + sh_ref[0]).astype(o_ref.dtype)


def modulate(x, scale, shift, g, out_dtype):
    B, T, D = x.shape
    tm = _row_tile(T, 512)
    tmod = tm if scale.shape[1] == T else 1
    mod_map = (lambda b, i: (b, i, 0)) if scale.shape[1] == T else (lambda b, i: (b, 0, 0))
    return pl.pallas_call(
        _modulate_kernel,
        grid=(B, T // tm),
        in_specs=[pl.BlockSpec((1, tm, D), lambda b, i: (b, i, 0)),
                  pl.BlockSpec((1, tmod, D), mod_map),
                  pl.BlockSpec((1, tmod, D), mod_map),
                  pl.BlockSpec((1, D), lambda b, i: (0, 0))],
        out_specs=pl.BlockSpec((1, tm, D), lambda b, i: (b, i, 0)),
        out_shape=jax.ShapeDtypeStruct((B, T, D), out_dtype),
        compiler_params=_params("parallel", "parallel"),
        name="modulate",
    )(x, scale, shift, g.reshape(1, D))


def _rmsnorm_kernel(x_ref, g_ref, o_ref):
    x = x_ref[...]
    o_ref[...] = x * lax.rsqrt(jnp.mean(x * x, axis=-1, keepdims=True) + EPS) * g_ref[...]


def _transpose_cast_kernel(x_ref, o_ref):
    o_ref[...] = x_ref[...].T.astype(o_ref.dtype)


def transpose_cast(x, dtype):
    R, C = x.shape
    tr = _row_tile(R, 1024)
    return pl.pallas_call(
        _transpose_cast_kernel,
        grid=(R // tr,),
        in_specs=[pl.BlockSpec((tr, C), lambda i: (i, 0))],
        out_specs=pl.BlockSpec((C, tr), lambda i: (0, i)),
        out_shape=jax.ShapeDtypeStruct((C, R), dtype),
        compiler_params=_params("parallel"),
        name="transpose_cast",
    )(x)


def rms_norm_rows(x, g):
    N, D = x.shape
    tm = _row_tile(N, 512)
    return pl.pallas_call(
        _rmsnorm_kernel,
        grid=(N // tm,),
        in_specs=[pl.BlockSpec((tm, D), lambda i: (i, 0)), pl.BlockSpec((1, D), lambda i: (0, 0))],
        out_specs=pl.BlockSpec((tm, D), lambda i: (i, 0)),
        out_shape=jax.ShapeDtypeStruct((N, D), F32),
        compiler_params=_params("parallel"),
        name="final_norm",
    )(x, g.reshape(1, D))


def _mm_kernel(a_ref, w_ref, o_ref):
    o_ref[...] = jnp.dot(a_ref[...], w_ref[...], preferred_element_type=F32).astype(o_ref.dtype)


def _mm3_kernel(a_ref, w_ref, b_ref, o_ref):
    o_ref[0] = _dot3(a_ref[...], w_ref[0], _NN) + b_ref[0]


def matmul(a, w, out_dtype=F32, tn_want=1024):
    N, K = a.shape
    C = w.shape[1]
    tm = _row_tile(N, 512)
    tn = _row_tile(C, tn_want)
    return pl.pallas_call(
        _mm_kernel,
        grid=(C // tn, N // tm),
        in_specs=[pl.BlockSpec((tm, K), lambda j, i: (i, 0)), pl.BlockSpec((K, tn), lambda j, i: (0, j))],
        out_specs=pl.BlockSpec((tm, tn), lambda j, i: (i, j)),
        out_shape=jax.ShapeDtypeStruct((N, C), out_dtype),
        compiler_params=_params("parallel", "parallel"),
        name="matmul",
    )(a, w)


def ada_matmul(c, w, b):
    M, D = c.shape
    L, _, C = w.shape
    tn = _row_tile(C, 1536)
    return pl.pallas_call(
        _mm3_kernel,
        grid=(L, C // tn),
        in_specs=[pl.BlockSpec((M, D), lambda l, j: (0, 0)),
                  pl.BlockSpec((1, D, tn), lambda l, j: (l, 0, j)),
                  pl.BlockSpec((1, 1, tn), lambda l, j: (l, 0, j))],
        out_specs=pl.BlockSpec((1, M, tn), lambda l, j: (l, 0, j)),
        out_shape=jax.ShapeDtypeStruct((L, M, C), F32),
        compiler_params=_params("parallel", "parallel"),
        name="ada_matmul",
    )(c, w, b.reshape(L, 1, C))


def _paged_cast_kernel(tbl_ref, p0_ref, p1_ref, tail_ref, o_ref, *mean_ref, n_steps, mean_cols):
    j = pl.program_id(1)

    def emit(rows):
        o_ref[0] = rows.astype(o_ref.dtype)
        if mean_ref:
            mean_ref[0][0] = jnp.mean(rows[:, :mean_cols], axis=0, keepdims=True)

    @pl.when(j < n_steps)
    def _():
        emit(jnp.concatenate([p0_ref[0, 0], p1_ref[0, 0]], axis=0))

    @pl.when(j >= n_steps)
    def _():
        emit(tail_ref[0])


def paged_cast(pages, layer, table, tail, mean_cols=0):
    _, _, page, R = pages.shape
    B, n_pg = table.shape
    assert page == PAGE_SIZE and MOBA_BLOCK == 2 * PAGE_SIZE and n_pg % 2 == 0
    n_steps = n_pg // 2
    has_tail = tail is not None
    n_blk = n_steps + (1 if has_tail else 0)
    if not has_tail:
        tail = jnp.zeros((B, MOBA_BLOCK, R), F32)
    last = n_steps - 1

    def page_map(k):
        return lambda b, j, tbl: (layer, tbl[b, 2 * jnp.minimum(j, last) + k], 0, 0)

    out_shape = [jax.ShapeDtypeStruct((B, n_blk * MOBA_BLOCK, R), BF16)]
    out_specs = [pl.BlockSpec((1, MOBA_BLOCK, R), lambda b, j, tbl: (b, j, 0))]
    if mean_cols:
        out_shape.append(jax.ShapeDtypeStruct((B * n_blk, 1, mean_cols), F32))
        out_specs.append(pl.BlockSpec((1, 1, mean_cols), lambda b, j, tbl: (b * n_blk + j, 0, 0)))
    res = pl.pallas_call(
        functools.partial(_paged_cast_kernel, n_steps=n_steps, mean_cols=mean_cols),
        grid_spec=pltpu.PrefetchScalarGridSpec(
            num_scalar_prefetch=1,
            grid=(B, n_blk),
            in_specs=[pl.BlockSpec((1, 1, PAGE_SIZE, R), page_map(0)),
                      pl.BlockSpec((1, 1, PAGE_SIZE, R), page_map(1)),
                      pl.BlockSpec((1, MOBA_BLOCK, R), lambda b, j, tbl: (b, 0, 0))],
            out_specs=out_specs),
        out_shape=out_shape,
        compiler_params=_params("parallel", "arbitrary"),
        name="paged_cast",
    )(table, pages, pages, tail)
    if mean_cols:
        return res[0], res[1].reshape(B, n_blk, mean_cols)
    return res[0]


MOBA_CAUSAL_CLASSES = 8
DSA_CAUSAL_CLASSES = 4


def _prefix_lengths(L, n):
    while L % (n * MOBA_BLOCK):
        n //= 2
    return [L * (c + 1) // n for c in range(n)]


def _for_visible_prefix(L, n_classes, last_pos, body):
    prev = 0
    for lc in _prefix_lengths(L, n_classes):
        pl.when((last_pos >= prev) & (last_pos < lc))(functools.partial(body, lc))
        prev = lc


def _moba_kernel(q_ref, k_ref, v_ref, km_ref, o_ref, *, pos0, tq):
    i = pl.program_id(2)
    body = functools.partial(_moba_body, q_ref, k_ref, v_ref, km_ref, o_ref, pos0 + i * tq, tq)
    _for_visible_prefix(k_ref.shape[1], MOBA_CAUSAL_CLASSES, pos0 + (i + 1) * tq - 1, body)


def _moba_body(q_ref, k_ref, v_ref, km_ref, o_ref, q_pos0, tq, L):
    nbp = km_ref.shape[1]
    q = q_ref[0] * (D_HEAD ** -0.5)
    k = k_ref[0, :L]
    v = v_ref[0, :L]
    km = km_ref[0]
    lane = lax.broadcasted_iota(jnp.int32, (1, LANES), 1)
    q_pos = q_pos0 + lax.broadcasted_iota(jnp.int32, (tq, 1), 0)
    own = q_pos // MOBA_BLOCK
    key_pos = lax.broadcasted_iota(jnp.int32, (1, L), 1)
    causal = jnp.where(key_pos > q_pos, NEG, 0.0)
    blk_id = lax.broadcasted_iota(jnp.int32, (tq, nbp), 1).astype(F32)
    own = own.astype(F32)
    expand =jnp.where(lax.broadcasted_iota(jnp.int32, (nbp, L), 1) // MOBA_BLOCK
                       == lax.broadcasted_iota(jnp.int32, (nbp, L), 0), 1.0, 0.0).astype(BF16)
    past = blk_id < own
    outs = []
    for h in range(2):
        qh = jnp.where(lane // D_HEAD == h, q, 0.0)
        g = jnp.where(past, _dot3(qh, km, _NT), NEG)
        closed = jnp.where(blk_id == own, 0.0, NEG)
        for _ in range(MOBA_TOPK):
            m = jnp.max(g, axis=-1, keepdims=True)
            first = jnp.min(jnp.where(g == m, blk_id, nbp), axis=-1, keepdims=True)
            pick = blk_id == first
            closed = jnp.where(pick & past, 0.0, closed)
            g = jnp.where(pick, -jnp.inf, g)
        s = _dot_bf16(qh, k, _NT) + _dot_bf16(closed, expand) + causal
        p = jnp.exp(s - jnp.max(s, axis=-1, keepdims=True))
        o = _dot_bf16(p, v) / jnp.sum(p, axis=-1, keepdims=True)
        outs.append(o)
    o_ref[0] = jnp.where(lane // D_HEAD == 0, outs[0], outs[1])


def moba_attention(q3, kv, km, pos0):
    B, T, _ = q3.shape
    L = kv.shape[1]
    nbp = km.shape[1]
    tq = _row_tile(T, 128)
    n_pair = kv.shape[2] // (2 * LANES)
    return pl.pallas_call(
        functools.partial(_moba_kernel, pos0=pos0, tq=tq),
        grid=(B, n_pair, T // tq),
        in_specs=[pl.BlockSpec((1, tq, LANES), lambda b, hp, i: (b, i, hp)),
                  pl.BlockSpec((1, L, LANES), lambda b, hp, i: (b, 0, hp)),
                  pl.BlockSpec((1, L, LANES), lambda b, hp, i: (b, 0, n_pair + hp)),
                  pl.BlockSpec((1, nbp, LANES), lambda b, hp, i: (b, 0, hp))],
        out_specs=pl.BlockSpec((1, tq, LANES), lambda b, hp, i: (b, i, hp)),
        out_shape=jax.ShapeDtypeStruct((B, T, n_pair * LANES), F32),
        compiler_params=_params("parallel", "parallel", "parallel"),
        name="moba_attention",
    )(q3, kv, kv, km)


DSA_ROW = 2 * C_KV_HEADS * D_HEAD + IDX_DIM


def _dsa_kernel(qc_ref, qi_ref, wi_ref, kv_ref, o_ref, key_full, bias_full, *, pos0, tq, topk, n_heads):
    i = pl.program_id(1)
    body = functools.partial(_dsa_body, qc_ref, qi_ref, wi_ref, kv_ref, o_ref, key_full, bias_full,
                             pos0 + i * tq, tq, topk, n_heads)
    _for_visible_prefix(kv_ref.shape[1], DSA_CAUSAL_CLASSES, pos0 + (i + 1) * tq - 1, body)


def _dsa_body(qc_ref, qi_ref, wi_ref, kv_ref, o_ref, key_full, bias_full, q_pos0, tq, topk, n_heads, L):
    key_scr = key_full.at[:, :L]
    bias_scr = bias_full.at[:, :L]
    kvw = C_KV_HEADS * D_HEAD
    k2 = kv_ref[0, :L, 0:kvw]
    v2 = kv_ref[0, :L, kvw:2 * kvw]
    ki = kv_ref[0, :L, 2 * kvw:2 * kvw + IDX_DIM]
    qc = qc_ref[0] * (D_HEAD ** -0.5)
    qi = qi_ref[0]
    wi = wi_ref[0][:, DSA_ROW:DSA_ROW + IDX_HEADS] * (IDX_HEADS ** -0.5)
    q_pos = q_pos0 + lax.broadcasted_iota(jnp.int32, (tq, 1), 0)
    key_pos = lax.broadcasted_iota(jnp.int32, (1, L), 1)
    valid = key_pos <= q_pos

    score = jnp.zeros((tq, L), F32)
    for h in range(IDX_HEADS):
        lg = _dot_bf16(qi[:, h * IDX_DIM:(h + 1) * IDX_DIM], ki, _NT)
        score = score + jnp.maximum(lg, 0.0) * wi[:, h:h + 1]
    _dsa_select(score, valid, q_pos, None, key_scr, bias_scr, topk)

    bias = bias_scr[...]
    grp = n_heads // C_KV_HEADS
    zero = jnp.zeros((tq, D_HEAD), F32)
    outs = []
    for h in range(n_heads):
        g = h // grp
        qh = qc[:, h * D_HEAD:(h + 1) * D_HEAD]
        q128 = jnp.concatenate([qh, zero] if g == 0 else [zero, qh], axis=1)
        s = _dot_bf16(q128, k2, _NT) + bias
        p = jnp.exp(s - jnp.max(s, axis=-1, keepdims=True))
        o = _dot_bf16(p, v2) / jnp.sum(p, axis=-1, keepdims=True)
        outs.append(o[:, g * D_HEAD:(g + 1) * D_HEAD])
    o_ref[0] = jnp.concatenate(outs, axis=1)


def _dsa_select(score, valid, q_pos, row_ok, key_scr, bias_scr, topk):
    tq, L = score.shape
    score = jnp.where(valid, score, NEG)
    score = jnp.where(score == 0.0, 0.0, score)
    bits = pltpu.bitcast(score, jnp.int32)
    key = jnp.where(bits < 0, bits ^ 0x7FFFFFFF, bits)
    key_scr[...] = key

    def count_ge(c):
        return jnp.sum(jnp.where(key_scr[...] >= c, 1.0, 0.0), axis=-1, keepdims=True)

    t0 = jnp.where(count_ge(jnp.zeros((tq, 1), jnp.int32)) >= topk, 0, INT_MIN).astype(jnp.int32)

    def bit_step(it, t):
        c = t | lax.shift_left(jnp.int32(1), 30 - it)
        return jnp.where(count_ge(c) >= topk, c, t)

    thr = lax.fori_loop(0, 31, bit_step, t0)
    key = key_scr[...]
    take = (key >= thr) & valid
    bias_scr[...] = jnp.where(take, 0.0, NEG)
    n_take = jnp.sum(jnp.where(take, 1.0, 0.0), axis=-1, keepdims=True)
    if row_ok is not None:
        n_take = jnp.where(row_ok, n_take, 0.0)

    @pl.when(jnp.max(n_take) > topk)
    def _():
        above = (key > thr) & valid
        need = topk - jnp.sum(jnp.where(above, 1.0, 0.0), axis=-1, keepdims=True)
        upper = jnp.where(lax.broadcasted_iota(jnp.int32, (LANES, LANES), 0)
                          < lax.broadcasted_iota(jnp.int32, (LANES, LANES), 1), 1.0, 0.0).astype(BF16)

        def blk(j, seen):
            sl = pl.ds(pl.multiple_of(j * LANES, LANES), LANES)
            kj = key_scr[:, sl]
            ok = (j * LANES + lax.broadcasted_iota(jnp.int32, (1, LANES), 1)) <= q_pos
            tied = jnp.where((kj == thr) & ok, 1.0, 0.0)
            before = seen + _dot_bf16(tied, upper)
            keep = ((kj > thr) & ok) | ((tied > 0.0) & (before < need))
            bias_scr[:, sl] = jnp.where(keep, 0.0, NEG)
            return seen + jnp.sum(tied, axis=-1, keepdims=True)

        lax.fori_loop(0, L // LANES, blk, jnp.zeros((tq, 1), F32))


def _moba_decode_kernel(tbl_ref, *refs, n_steps, n_blocks, n_heads):
    per_step = n_blocks // n_steps
    page_refs = refs[:2 * per_step]
    tail_ref, q_ref, o_ref, o_scr, m_scr, l_scr, g_scr = refs[2 * per_step:]
    j = pl.program_id(1)
    W = n_heads * D_HEAD
    R = q_ref.shape[1]
    lane = lax.broadcasted_iota(jnp.int32, (1, LANES), 1)
    head_cols = (lax.broadcasted_iota(jnp.int32, (R, W), 1) // D_HEAD
                 == lax.broadcasted_iota(jnp.int32, (R, W), 0) % n_heads)
    qm = jnp.where(head_cols, q_ref[0], 0.0)
    qs = qm * (D_HEAD ** -0.5)

    @pl.when(j == 0)
    def _():
        m_scr[...] = jnp.zeros(m_scr.shape, F32)
        l_scr[...] = jnp.zeros(l_scr.shape, F32)
        g_scr[...] = jnp.full(g_scr.shape, NEG, F32)

    @pl.when(j < n_steps)
    def _():
        for i in range(per_step):
            jb = j * per_step + i
            rows = jnp.concatenate([page_refs[2 * i][0, 0], page_refs[2 * i + 1][0, 0]], axis=0)
            k, v = rows[:, :W], rows[:, W:]
            gate = jnp.sum(qm * jnp.mean(k, axis=0, keepdims=True), axis=-1, keepdims=True)
            s = _dot_bf16(qs, k, _NT)
            m = jnp.max(s, axis=-1, keepdims=True)
            p = jnp.exp(s - m)
            o_scr[jb] = _dot_bf16(p, v)
            here = lane == jb
            m_scr[...] = jnp.where(here, m, m_scr[...])
            l_scr[...] = jnp.where(here, jnp.sum(p, axis=-1, keepdims=True), l_scr[...])
            g_scr[...] = jnp.where(here, gate, g_scr[...])

    @pl.when(j == n_steps)
    def _():
        tail = tail_ref[0]
        s = _dot_bf16(qs, tail[:, :W], _NT)
        n_t = tail.shape[0]
        seen = (lax.broadcasted_iota(jnp.int32, (R, n_t), 1)
                <= lax.broadcasted_iota(jnp.int32, (R, n_t), 0) // n_heads)
        s = jnp.where(seen, s, NEG)
        m_own = jnp.max(s, axis=-1, keepdims=True)
        p = jnp.exp(s - m_own)
        l_own = jnp.sum(p, axis=-1, keepdims=True)
        o_own = _dot_bf16(p, tail[:, W:])
        blk = lane.astype(F32)
        g = g_scr[...]
        sel = jnp.zeros(g.shape, F32)
        for _ in range(MOBA_TOPK):
            mx = jnp.max(g, axis=-1, keepdims=True)
            first = jnp.min(jnp.where(g == mx, blk, float(LANES)), axis=-1, keepdims=True)
            pick = blk == first
            sel = jnp.where(pick & (lane < n_blocks), 1.0, sel)
            g = jnp.where(pick, -jnp.inf, g)
        m_all = jnp.maximum(m_own, jnp.max(jnp.where(sel > 0.0, m_scr[...], -jnp.inf), axis=-1, keepdims=True))
        wgt = jnp.where(sel > 0.0, jnp.exp(m_scr[...] - m_all), 0.0)
        w_own = jnp.exp(m_own - m_all)
        denom = jnp.sum(wgt * l_scr[...], axis=-1, keepdims=True) + w_own * l_own

        def add_block(jj, acc):
            return acc + jnp.sum(jnp.where(lane == jj, wgt, 0.0), axis=-1, keepdims=True) * o_scr[jj]

        acc = lax.fori_loop(0, n_blocks, add_block, w_own * o_own)
        res = jnp.where(head_cols, acc / denom, 0.0)
        o_ref[0] = jnp.sum(res.reshape(R // n_heads, n_heads, W), axis=1)


def moba_decode(q, pages, layer, table, new_rows):
    B, T, W = q.shape
    n_heads = W // D_HEAD
    n_pg = table.shape[1]
    assert n_pg % 2 == 0 and n_pg // 2 <= LANES and T <= 16
    n_blocks = n_pg // 2
    per_step = 2 if n_blocks % 2 == 0 else 1
    n_steps = n_blocks // per_step
    ppb = 2 * per_step
    last = n_steps - 1
    R2 = pages.shape[3]
    q_rows = jnp.repeat(q, n_heads, axis=1)
    tail = jnp.pad(new_rows, ((0, 0), (0, 16 - T), (0, 0)))

    def page_map(k):
        return lambda b, j, tbl: (layer, tbl[b, ppb * jnp.minimum(j, last) + k], 0, 0)

    R = T * n_heads
    return pl.pallas_call(
        functools.partial(_moba_decode_kernel, n_steps=n_steps, n_blocks=n_blocks, n_heads=n_heads),
        grid_spec=pltpu.PrefetchScalarGridSpec(
            num_scalar_prefetch=1,
            grid=(B, n_steps + 1),
            in_specs=[pl.BlockSpec((1, 1, PAGE_SIZE, R2), page_map(k)) for k in range(ppb)]
                     + [pl.BlockSpec((1, 16, R2), lambda b, j, tbl: (b, 0, 0)),
                        pl.BlockSpec((1, R, W), lambda b, j, tbl: (b, 0, 0))],
            out_specs=pl.BlockSpec((1, T, W), lambda b, j, tbl: (b, 0, 0)),
            scratch_shapes=[pltpu.VMEM((n_blocks, R, W), F32), pltpu.VMEM((R, LANES), F32),
                            pltpu.VMEM((R, LANES), F32), pltpu.VMEM((R, LANES), F32)]),
        out_shape=jax.ShapeDtypeStruct((B, T, W), F32),
        compiler_params=_params("parallel", "arbitrary"),
        name="moba_decode",
    )(table, *([pages] * ppb), tail, q_rows)


def _dsa_decode_kernel(tbl_ref, *refs, n_steps, pps, t_new, topk, n_heads, pos0):
    page_refs = refs[:pps]
    tail_ref, qi_ref, wi_ref, qc_ref, o_ref, kv_scr, sc_scr, key_scr, bias_scr = refs[pps:]
    j = pl.program_id(1)
    kvw = C_KV_HEADS * D_HEAD
    TQ = SUBLANES

    def take(page_t, at):
        kv_scr[:, at] = page_t[:2 * kvw].astype(BF16)
        lg = _dot_bf16(qi_ref[0], page_t[2 * kvw:2 * kvw + IDX_DIM])
        sc = jnp.maximum(lg, 0.0) * wi_ref[0]
        sc_scr[:, at] = jnp.sum(sc.reshape(TQ, IDX_HEADS, PAGE_SIZE), axis=1)

    @pl.when(j < n_steps)
    def _():
        for k, p_ref in enumerate(page_refs):
            take(p_ref[0, 0], pl.ds(pl.multiple_of((j * pps + k) * PAGE_SIZE, PAGE_SIZE), PAGE_SIZE))

    @pl.when(j == n_steps)
    def _():
        L = sc_scr.shape[1]
        take(tail_ref[0], pl.ds(n_steps * pps * PAGE_SIZE, PAGE_SIZE))
        row = lax.broadcasted_iota(jnp.int32, (TQ, 1), 0)
        q_pos = pos0 + row
        valid = lax.broadcasted_iota(jnp.int32, (1, L), 1) <= q_pos
        _dsa_select(sc_scr[...], valid, q_pos, row < t_new, key_scr, bias_scr, topk)
        bias = jnp.broadcast_to(bias_scr[...][None], (n_heads, TQ, L)).reshape(n_heads * TQ, L)
        s = _dot_bf16(qc_ref[0] * (D_HEAD ** -0.5), kv_scr[:kvw, :]) + bias
        p = jnp.exp(s - jnp.max(s, axis=-1, keepdims=True))
        o = _dot_bf16(p, kv_scr[kvw:, :], _NT) / jnp.sum(p, axis=-1, keepdims=True)
        grp = n_heads // C_KV_HEADS
        o_ref[0] = jnp.concatenate(
            [o[h * TQ:(h + 1) * TQ, (h // grp) * D_HEAD:(h // grp + 1) * D_HEAD] for h in range(n_heads)], axis=1)


def dsa_decode(qc, qi, wi, pages, layer, table, new_rows, pos0):
    B, T, W = qc.shape
    n_heads = W // D_HEAD
    n_pg = table.shape[1]
    assert n_pg % 2 == 0 and T <= SUBLANES and pos0 == n_pg * PAGE_SIZE
    pps = 4 if n_pg % 4 == 0 else 2
    n_steps = n_pg // pps
    last = n_steps - 1
    TQ = SUBLANES
    L = n_pg * PAGE_SIZE + LANES
    kvw = C_KV_HEADS * D_HEAD
    grp = n_heads // C_KV_HEADS
    padq = lambda a: jnp.pad(a, ((0, 0), (0, TQ - T), (0, 0)))
    qi_rows = padq(qi).reshape(B, TQ * IDX_HEADS, IDX_DIM)
    wi_rows = jnp.broadcast_to((padq(wi) * IDX_HEADS ** -0.5).reshape(B, TQ * IDX_HEADS, 1),
                               (B, TQ * IDX_HEADS, LANES))
    qh = padq(qc).reshape(B, TQ, n_heads, D_HEAD).transpose(0, 2, 1, 3)
    z = jnp.zeros_like(qh)
    first = (jnp.arange(n_heads) < grp)[None, :, None, None]
    qc_rows = jnp.concatenate([jnp.where(first, qh, z), jnp.where(first, z, qh)], axis=-1)
    qc_rows = qc_rows.reshape(B, n_heads * TQ, kvw)
    pages = jnp.swapaxes(pages, 2, 3)
    tail = jnp.pad(jnp.swapaxes(new_rows, 1, 2), ((0, 0), (0, 0), (0, PAGE_SIZE - T)))

    def page_map(k):
        return lambda b, j, tbl: (layer, tbl[b, pps * jnp.minimum(j, last) + k], 0, 0)

    R = pages.shape[2]
    fixed = lambda r, c: pl.BlockSpec((1, r, c), lambda b, j, tbl: (b, 0, 0))
    out = pl.pallas_call(
        functools.partial(_dsa_decode_kernel, n_steps=n_steps, pps=pps, t_new=T,
                          topk=min(DSA_TOPK, (pos0 + T) // 4), n_heads=n_heads, pos0=pos0),
        grid_spec=pltpu.PrefetchScalarGridSpec(
            num_scalar_prefetch=1,
            grid=(B, n_steps + 1),
            in_specs=[pl.BlockSpec((1, 1, R, PAGE_SIZE), page_map(k)) for k in range(pps)]
                     + [fixed(R, PAGE_SIZE), fixed(TQ * IDX_HEADS, IDX_DIM), fixed(TQ * IDX_HEADS, LANES),
                        fixed(n_heads * TQ, kvw)],
            out_specs=fixed(TQ, W),
            scratch_shapes=[pltpu.VMEM((2 * kvw, L), BF16), pltpu.VMEM((TQ, L), F32),
                            pltpu.VMEM((TQ, L), jnp.int32), pltpu.VMEM((TQ, L), F32)]),
        out_shape=jax.ShapeDtypeStruct((B, TQ, W), F32),
        compiler_params=_params("parallel", "arbitrary"),
        name="dsa_decode",
    )(table, *([pages] * pps), tail, qi_rows, wi_rows, qc_rows)
    return out[:, :T]


def dsa_attention(q3, drow, kv, pos0, n_keys):
    B, T, _ = q3.shape
    L, R = kv.shape[1], kv.shape[2]
    W = q3.shape[2] // 3
    tq = _row_tile(T, 128)
    topk = min(DSA_TOPK, n_keys // 4)
    return pl.pallas_call(
        functools.partial(_dsa_kernel, pos0=pos0, tq=tq, topk=topk, n_heads=W // D_HEAD),
        grid=(B, T // tq),
        in_specs=[pl.BlockSpec((1, tq, W), lambda b, i: (b, i, 1)),
                  pl.BlockSpec((1, tq, W), lambda b, i: (b, i, 2)),
                  pl.BlockSpec((1, tq, drow.shape[2]), lambda b, i: (b, i, 0)),
                  pl.BlockSpec((1, L, R), lambda b, i: (b, 0, 0))],
        out_specs=pl.BlockSpec((1, tq, W), lambda b, i: (b, i, 0)),
        out_shape=jax.ShapeDtypeStruct((B, T, W), F32),
        scratch_shapes=[pltpu.VMEM((tq, L), jnp.int32), pltpu.VMEM((tq, L), F32)],
        compiler_params=_params("parallel", "parallel"),
        name="dsa_attention",
    )(q3, q3, drow, kv)


def _segsum(x, ones_blk):
    hi = x.astype(BF16)
    r1 = x - hi.astype(F32)
    mid = r1.astype(BF16)
    lo = (r1 - mid.astype(F32)).astype(BF16)
    d = lambda a: jnp.dot(a, ones_blk, preferred_element_type=F32)
    return d(hi) + (d(mid) + d(lo))


def _rwkv_prep_kernel(z_ref, zs_ref, mu_ref, lora_ref, vec_ref, ones_ref, rs_ref, gb_ref, *t3_ref, width):
    W = width
    z = z_ref[0]
    xm = z + (zs_ref[0] - z) * mu_ref[...]
    r, k, v, g = (xm[:, n * W:(n + 1) * W] for n in range(4))
    lat = xm[:, 4 * W:4 * W + 2 * B_LORA]
    lane = lax.broadcasted_iota(jnp.int32, (1, 2 * B_LORA), 1)
    lat = jnp.where(lane < B_LORA, jnp.tanh(lat), lat)
    up = _dot3(lat, lora_ref[...], _NN)
    w0, a0, k_k, k_a, r_k = (vec_ref[n:n + 1, :] for n in range(5))
    wpre = -(w0 + up[:, :W])
    softplus = jnp.maximum(wpre, 0.0) + jnp.log(1.0 + jnp.exp(-jnp.abs(wpre)))
    log_decay = -jnp.exp(-softplus - 0.5)
    a = jax.nn.sigmoid(a0 + up[:, W:])
    ones_blk = ones_ref[...]
    kk = k * k_k
    kk = kk / jnp.maximum(jnp.sqrt(_segsum(kk * kk, ones_blk)), 1e-12)
    k = k * (1.0 + (a - 1.0) * k_a)
    kka = kk * a
    bonus = _segsum(r * k * r_k, ones_blk) * v
    rs_ref[0] = jnp.concatenate([r, jnp.exp(log_decay), k, kk, kka, v, log_decay], axis=1)
    gb_ref[0] = jnp.concatenate([jax.nn.sigmoid(g), bonus], axis=1)
    if t3_ref:
        t3_ref[0][0] = jnp.concatenate([log_decay.T, kka.T, k.T], axis=0)


RS_SEGMENTS = 7


def rwkv_prep(zr, zs, mu, lora, vecs, ones_blk, key_major):
    B, T, C = zr.shape
    W = vecs.shape[1]
    tm = _row_tile(T, 256)
    full = lambda a: pl.BlockSpec(a.shape, lambda b, i: (0,) * a.ndim)
    row = lambda c: pl.BlockSpec((1, tm, c), lambda b, i: (b, i, 0))
    out_specs = [row(RS_SEGMENTS * W), row(2 * W)]
    out_shape = [jax.ShapeDtypeStruct((B, T, RS_SEGMENTS * W), F32), jax.ShapeDtypeStruct((B, T, 2 * W), F32)]
    if key_major:
        out_specs.append(pl.BlockSpec((1, 3 * W, tm), lambda b, i: (b, 0, i)))
        out_shape.append(jax.ShapeDtypeStruct((B, 3 * W, T), F32))
    return pl.pallas_call(
        functools.partial(_rwkv_prep_kernel, width=W),
        grid=(B, T // tm),
        in_specs=[row(C), row(C), full(mu), full(lora), full(vecs), full(ones_blk)],
        out_specs=out_specs,
        out_shape=out_shape,
        compiler_params=_params("parallel", "parallel"),
        name="rwkv_prep",
    )(zr, zs, mu, lora, vecs, ones_blk)


RWKV_CHUNK = 64


def _split(a):
    return _split_bf16(a)


def _dot3s(a, b, dims=_NN):
    dn = (dims, ((), ()))
    d = lambda x, y: lax.dot_general(x, y, dn, preferred_element_type=F32)
    return d(a[0], b[0]) + (d(a[0], b[1]) + d(a[1], b[0]))


def _rows(s, lo, hi):
    return s[0][lo:hi], s[1][lo:hi]


def _rwkv_chunk_kernel(rs_ref, t3_ref, gb_ref, s0_ref, ln_ref, y_ref, sn_ref, st_scr, *, bb, width):
    c = pl.program_id(1)
    W = width
    n_hp = W // LANES
    C = RWKV_CHUNK
    TL = 2 * C
    assert TL == LANES and C == D_HEAD

    @pl.when(c == 0)
    def _():
        st_scr[...] = s0_ref[...]

    ri = lax.broadcasted_iota(jnp.int32, (TL, TL), 0)
    ci = lax.broadcasted_iota(jnp.int32, (TL, TL), 1)
    same = (ri // C) == (ci // C)
    strict = same & (ci < ri)
    incl = same & (ci <= ri)
    low_blk = jnp.where(incl, 1.0, 0.0).astype(BF16)
    up_blk = jnp.where(same & (ri <= ci), 1.0, 0.0).astype(BF16)
    ones_blk = jnp.where(same, 1.0, 0.0).astype(BF16)
    eye = jnp.where(ri == ci, 1.0, 0.0)
    lane = lax.broadcasted_iota(jnp.int32, (1, TL), 1)
    left = lane < C
    zeros_c = jnp.zeros((C, TL), F32)

    def exact3(x, w, x_first):
        hi = x.astype(BF16)
        r1 = x - hi.astype(F32)
        mid = r1.astype(BF16)
        lo = (r1 - mid.astype(F32)).astype(BF16)
        d = (lambda a: jnp.dot(a, w, preferred_element_type=F32)) if x_first else \
            (lambda a: jnp.dot(w, a, preferred_element_type=F32))
        return d(hi) + (d(mid) + d(lo))

    pairs = []
    for p in range(bb * n_hp):
        b, hp = divmod(p, n_hp)
        seg = lambda n: rs_ref[b, :, pl.ds(n * W + hp * LANES, LANES)]
        r, k, kk, kka, v, lw = seg(0), seg(2), seg(3), seg(4), seg(5), seg(6)
        lw_t, kka_t, k_t = (t3_ref[b, pl.ds(n * W + hp * LANES, LANES), :] for n in range(3))
        cum = exact3(lw, low_blk, False)
        cum_t = exact3(lw_t, up_blk, True)
        a_til = -kk * jnp.exp(cum - lw)
        r_til = r * jnp.exp(cum)
        inv = jnp.exp(-cum)
        bk = _split(jnp.concatenate([kka * inv, k * inv], axis=0))
        cend = jnp.where(left, cum_t[:, C - 1:C], cum_t[:, TL - 1:TL])
        rest = jnp.exp(cend - cum_t)
        d = dict(b=b, hp=hp, v=v, vs=_split(v), a_til=a_til, r_til=r_til, cum_t=cum_t,
                 bk_hat_t=jnp.concatenate([kka_t * rest, k_t * rest], axis=1),
                 lak=[], mrbk=[], x=[], t=[])
        for h in range(2):
            mh = (lane // C) == h
            ar = _split(jnp.concatenate([jnp.where(mh, a_til, 0.0), jnp.where(mh, r_til, 0.0)], axis=0))
            m = _dot3s(ar, bk, _NT)
            lab = jnp.where(strict, m[:TL, :TL], 0.0)
            d["lak"].append(_split(jnp.where(strict, m[:TL, TL:], 0.0)))
            d["mrbk"].append(_split(jnp.concatenate([jnp.where(incl, m[TL:, :TL], 0.0),
                                                     jnp.where(incl, m[TL:, TL:], 0.0)], axis=1)))
            d["x"].append(_split(lab))
            d["t"].append(eye + lab)
        pairs.append(d)

    for _ in range(5):
        for d in pairs:
            for h in range(2):
                d["x"][h] = _split(_dot3s(d["x"][h], d["x"][h]))
                d["t"][h] = d["t"][h] + _dot3s(d["x"][h], _split(d["t"][h]))
    for p, d in enumerate(pairs):
        d["tinv"] = [_split(t) for t in d["t"]]
        d["st"] = st_scr[p]
        d["ys"] = []
        d["u_ext"] = None

    for s in range(2):
        lo, hi = s * C, (s + 1) * C
        in_chunk = (lax.broadcasted_iota(jnp.int32, (1, 2 * TL), 1) % TL) // C == s
        for d in pairs:
            lak, tinv, mrbk, vs = d["lak"], d["tinv"], d["mrbk"], d["vs"]
            sts = _split(d["st"])
            ars = _dot3s(_split(jnp.concatenate([d["a_til"][lo:hi], d["r_til"][lo:hi]], axis=0)), sts)
            rhs = ars[:C] + jnp.where(left, _dot3s(_rows(lak[0], lo, hi), vs), _dot3s(_rows(lak[1], lo, hi), vs))
            rhs_ext = _split(jnp.concatenate([rhs, zeros_c] if s == 0 else [zeros_c, rhs], axis=0))
            u = jnp.where(left, _dot3s(_rows(tinv[0], lo, hi), rhs_ext), _dot3s(_rows(tinv[1], lo, hi), rhs_ext))
            d["u_ext"] = jnp.concatenate([u, zeros_c] if s == 0 else [d["u_ext"][:C], u], axis=0)
            uv = _split(jnp.concatenate([d["u_ext"], d["v"]], axis=0))
            d["ys"].append(ars[C:] + jnp.where(left, _dot3s(_rows(mrbk[0], lo, hi), uv),
                                               _dot3s(_rows(mrbk[1], lo, hi), uv)))
            upd = _dot3s(_split(jnp.where(in_chunk, d["bk_hat_t"], 0.0)), uv)
            d["st"] = jnp.where(same, jnp.exp(d["cum_t"][:, hi - 1:hi]) * d["st"] + upd, 0.0)

    for p, d in enumerate(pairs):
        st_scr[p] = d["st"]
        b, hp = d["b"], d["hp"]
        y = jnp.concatenate(d["ys"], axis=0)
        mean = exact3(y, ones_blk, True) * (1.0 / C)
        var = exact3(jnp.square(y - mean), ones_blk, True) * (1.0 / C)
        cols = pl.ds(hp * LANES, LANES)
        yn = (y - mean) * lax.rsqrt(var + RWKV_GN_EPS) * ln_ref[0:1, cols] + ln_ref[1:2, cols]
        y_ref[b, :, cols] = (yn + gb_ref[b, :, pl.ds(W + hp * LANES, LANES)]) * gb_ref[b, :, cols]

    @pl.when(c == pl.num_programs(1) - 1)
    def _():
        sn_ref[...] = st_scr[...]


def rwkv_chunk_scan(rs, t3, gb, s0, ln_w, ln_b):
    B, T, _ = rs.shape
    W = gb.shape[2] // 2
    n_hp = W // LANES
    TL = 2 * RWKV_CHUNK
    bb = 2 if B % 2 == 0 else 1
    assert T % TL == 0
    ln = jnp.concatenate([ln_w.reshape(1, W), ln_b.reshape(1, W), jnp.zeros((SUBLANES - 2, W), F32)], axis=0)
    st_spec = pl.BlockSpec((bb * n_hp, LANES, LANES), lambda g, c: (g, 0, 0))
    return pl.pallas_call(
        functools.partial(_rwkv_chunk_kernel, bb=bb, width=W),
        grid=(B // bb, T // TL),
        in_specs=[pl.BlockSpec((bb, TL, RS_SEGMENTS * W), lambda g, c: (g, c, 0)),
                  pl.BlockSpec((bb, 3 * W, TL), lambda g, c: (g, 0, c)),
                  pl.BlockSpec((bb, TL, 2 * W), lambda g, c: (g, c, 0)),
                  st_spec,
                  pl.BlockSpec((SUBLANES, W), lambda g, c: (0, 0))],
        out_specs=[pl.BlockSpec((bb, TL, W), lambda g, c: (g, c, 0)), st_spec],
        out_shape=[jax.ShapeDtypeStruct((B, T, W), F32),
                   jax.ShapeDtypeStruct((B * n_hp, LANES, LANES), F32)],
        scratch_shapes=[pltpu.VMEM((bb * n_hp, LANES, LANES), F32)],
        compiler_params=_params("parallel", "arbitrary"),
        name="rwkv_chunk_scan",
    )(rs, t3, gb, s0, ln)


def _rwkv_scan_kernel(rs_ref, gb_ref, s0_ref, lnw_ref, lnb_ref, y_ref, sn_ref, s_scr, yt_scr,
                      *, bb, tc, t_total, width):
    c = pl.program_id(1)
    n_hp = width // LANES
    n_pair = bb * n_hp

    @pl.when(c == 0)
    def _():
        s_scr[...] = s0_ref[...]

    yt_scr[...] = jnp.zeros(yt_scr.shape, F32)
    lane = lax.broadcasted_iota(jnp.int32, (1, LANES), 1)
    left = lane < D_HEAD
    eye2 = jnp.where(lax.broadcasted_iota(jnp.int32, (D_HEAD, LANES), 0)
                     == lax.broadcasted_iota(jnp.int32, (D_HEAD, LANES), 1) % D_HEAD, 1.0, 0.0)
    tok = lax.broadcasted_iota(jnp.int32, (1, tc), 1)

    def halves(p):
        return (jnp.sum(jnp.where(left, p, 0.0), axis=-1, keepdims=True),
                jnp.sum(jnp.where(left, 0.0, p), axis=-1, keepdims=True))

    def step(g, carry):
        t0 = pl.multiple_of(g * SUBLANES, SUBLANES)
        for p in range(n_pair):
            b, hp = divmod(p, n_hp)
            grp = [rs_ref[b, pl.ds(t0, SUBLANES), pl.ds(n * width + hp * LANES, LANES)] for n in range(6)]
            s = s_scr[p]
            y_l, y_r = yt_scr[2 * p], yt_scr[2 * p + 1]
            for j in range(SUBLANES):
                r, w, k, kk, kka, v = (a[j:j + 1] for a in grp)
                sa_l, sa_r = halves(s * kk)
                v_l, v_r = halves(eye2 * v)
                s = s * w - jnp.where(left, sa_l, sa_r) * kka + jnp.where(left, v_l, v_r) * k
                o_l, o_r = halves(s * r)
                here = tok == t0 + j
                y_l = jnp.where(here, o_l, y_l)
                y_r = jnp.where(here, o_r, y_r)
            s_scr[p] = s
            yt_scr[2 * p], yt_scr[2 * p + 1] = y_l, y_r
        return carry

    n_tok = jnp.minimum(tc, t_total - c * tc)
    lax.fori_loop(0, (n_tok + SUBLANES - 1) // SUBLANES, step, 0)

    for p in range(n_pair):
        b, hp = divmod(p, n_hp)
        rows = []
        for h2 in range(2):
            y = yt_scr[2 * p + h2]
            mean = jnp.mean(y, axis=0, keepdims=True)
            var = jnp.mean(jnp.square(y - mean), axis=0, keepdims=True)
            seg = pl.ds(hp * LANES + h2 * D_HEAD, D_HEAD)
            rows.append((y - mean) * lax.rsqrt(var + RWKV_GN_EPS) * lnw_ref[seg, :] + lnb_ref[seg, :])
        yn = jnp.concatenate(rows, axis=0).T
        cols = pl.ds(hp * LANES, LANES)
        y_ref[b, :, cols] = (yn + gb_ref[b, :, pl.ds(width + hp * LANES, LANES)]) * gb_ref[b, :, cols]

    @pl.when(c == pl.num_programs(1) - 1)
    def _():
        sn_ref[...] = s_scr[...]


def rwkv_scan(rs, gb, s0, ln_w, ln_b, t_total):
    B, Tp, _ = rs.shape
    W = gb.shape[2] // 2
    n_hp = W // LANES
    tc = LANES
    bb = min(B, 4)
    assert B % bb == 0 and Tp % tc == 0
    lnw = jnp.broadcast_to(ln_w.reshape(W, 1), (W, tc))
    lnb = jnp.broadcast_to(ln_b.reshape(W, 1), (W, tc))
    return pl.pallas_call(
        functools.partial(_rwkv_scan_kernel, bb=bb, tc=tc, t_total=t_total, width=W),
        grid=(B // bb, Tp // tc),
        in_specs=[pl.BlockSpec((bb, tc, RS_SEGMENTS * W), lambda g, c: (g, c, 0)),
                  pl.BlockSpec((bb, tc, 2 * W), lambda g, c: (g, c, 0)),
                  pl.BlockSpec((bb * n_hp, D_HEAD, LANES), lambda g, c: (g, 0, 0)),
                  pl.BlockSpec((W, tc), lambda g, c: (0, 0)),
                  pl.BlockSpec((W, tc), lambda g, c: (0, 0))],
        out_specs=[pl.BlockSpec((bb, tc, W), lambda g, c: (g, c, 0)),
                   pl.BlockSpec((bb * n_hp, D_HEAD, LANES), lambda g, c: (g, 0, 0))],
        out_shape=[jax.ShapeDtypeStruct((B, Tp, W), F32),
                   jax.ShapeDtypeStruct((B * n_hp, D_HEAD, LANES), F32)],
        scratch_shapes=[pltpu.VMEM((bb * n_hp, D_HEAD, LANES), F32),
                        pltpu.VMEM((2 * bb * n_hp, D_HEAD, tc), F32)],
        compiler_params=_params("parallel", "arbitrary"),
        name="rwkv_scan",
    )(rs, gb, s0, lnw, lnb)


def _merge_kernel(ya_ref, yb_ref, yc_ref, gl_ref, x_ref, gt_ref, wb_ref, wo_ref, o_ref):
    D = x_ref.shape[2]
    mix = None
    for n, y_ref in enumerate((ya_ref, yb_ref, yc_ref)):
        proj = _dot_bf16(y_ref[0], wb_ref[n])
        gate = jax.nn.sigmoid(gl_ref[0, :, n * D:(n + 1) * D].astype(F32))
        mix = gate * proj if mix is None else mix + gate * proj
    o_ref[0] = x_ref[0] + gt_ref[0] * _dot_bf16(mix, wo_ref[...])


def merge(ya, yb, yc, gl, x, gt, wb, wo):
    B, T, D = x.shape
    W = ya.shape[2]
    tm = _row_tile(T, 256)
    tmod = tm if gt.shape[1] == T else 1
    mod_map = (lambda b, i: (b, i, 0)) if gt.shape[1] == T else (lambda b, i: (b, 0, 0))
    row = lambda c: pl.BlockSpec((1, tm, c), lambda b, i: (b, i, 0))
    return pl.pallas_call(
        _merge_kernel,
        grid=(B, T // tm),
        in_specs=[row(W), row(W), row(W), row(N_BRANCH * D), row(D), pl.BlockSpec((1, tmod, D), mod_map),
                  pl.BlockSpec(wb.shape, lambda b, i: (0, 0, 0)), pl.BlockSpec(wo.shape, lambda b, i: (0, 0))],
        out_specs=row(D),
        out_shape=jax.ShapeDtypeStruct((B, T, D), F32),
        compiler_params=_params("parallel", "parallel"),
        name="merge",
    )(ya, yb, yc, gl, x, gt, wb, wo)


def _top_values(x, n):
    vals = []
    for _ in range(n):
        m = jnp.max(x, axis=0, keepdims=True)
        vals.append(m)
        x = jnp.where(x == m, -jnp.inf, x)
    return vals


def _peer_route_kernel(x_ref, sc_ref, sh_ref, g_ref, wq_ref, keys_ref, ht_ref, s1_ref, s2_ref, e2_ref, st_ref):
    x = x_ref[0]
    h = x * lax.rsqrt(jnp.mean(x * x, axis=-1, keepdims=True) + EPS) * g_ref[...]
    h = h * (1.0 + sc_ref[0]) + sh_ref[0]
    ht_ref[...] = h.T.astype(BF16)
    q = _dot_bf16(h, wq_ref[...])
    tm = x.shape[0]
    for hd in range(PEER_HEADS):
        s1, s2 = (_dot_bf16(keys_ref[2 * hd + p], q[:, (2 * hd + p) * LANES:(2 * hd + p + 1) * LANES], _NT)
                  for p in range(2))
        t1 = _top_values(s1, PEER_TOPK)
        t2 = jnp.concatenate(_top_values(s2, PEER_TOPK), axis=0)
        half = PEER_TOPK // 2
        row = lax.broadcasted_iota(jnp.int32, (half, 1), 0)
        cand = [t1[0] + t2, t1[1] + t2[:half]]
        cand += [jnp.where(row < PEER_TOPK // (a + 1), t1[a] + t2[:half], -jnp.inf) for a in range(2, half)]
        cand.append(jnp.concatenate(t1[half:], axis=0) + t2[0:1])
        best = _top_values(jnp.concatenate(cand, axis=0), PEER_TOPK)
        z = sum(jnp.exp(b - best[0]) for b in best)
        s1_ref[hd] = s1
        s2_ref[hd] = s2
        e2_ref[hd] = jnp.exp(s2 - t2[0:1])
        st_ref[hd] = jnp.concatenate([best[-1], t2[0:1] - best[0], 1.0 / z, jnp.zeros((SUBLANES - 3, tm), F32)],
                                     axis=0)


def peer_route(x, scale, shift, g, wq, keys):
    B, T, D = x.shape
    N = B * T
    tm = _row_tile(T, 256)
    nt = T // tm
    tmod = tm if scale.shape[1] == T else 1
    mod_map = (lambda b, i: (b, i, 0)) if scale.shape[1] == T else (lambda b, i: (b, 0, 0))
    hkn = jax.ShapeDtypeStruct((PEER_HEADS, PEER_KEYS, N), F32)
    hk_spec = pl.BlockSpec((PEER_HEADS, PEER_KEYS, tm), lambda b, i: (0, 0, b * nt + i))
    return pl.pallas_call(
        _peer_route_kernel,
        grid=(B, nt),
        in_specs=[pl.BlockSpec((1, tm, D), lambda b, i: (b, i, 0)),
                  pl.BlockSpec((1, tmod, D), mod_map), pl.BlockSpec((1, tmod, D), mod_map),
                  pl.BlockSpec((1, D), lambda b, i: (0, 0)),
                  pl.BlockSpec(wq.shape, lambda b, i: (0, 0)),
                  pl.BlockSpec(keys.shape, lambda b, i: (0, 0, 0))],
        out_specs=[pl.BlockSpec((D, tm), lambda b, i: (0, b * nt + i)), hk_spec, hk_spec, hk_spec,
                   pl.BlockSpec((PEER_HEADS, SUBLANES, tm), lambda b, i: (0, 0, b * nt + i))],
        out_shape=[jax.ShapeDtypeStruct((D, N), BF16), hkn, hkn, hkn,
                   jax.ShapeDtypeStruct((PEER_HEADS, SUBLANES, N), F32)],
        compiler_params=_params("parallel", "parallel"),
        name="peer_route",
    )(x, scale, shift, g.reshape(1, D), wq, keys)


def _peer_expert_kernel(ht_ref, s1_ref, s2_ref, e2_ref, st_ref, u_ref, vt_ref, x_ref, gt_ref, o_ref, acc_ref):
    e = pl.program_id(2)

    @pl.when(e == 0)
    def _():
        acc_ref[...] = jnp.zeros(acc_ref.shape, F32)

    ht = ht_ref[...]
    upd = None
    group = 2 * PEER_KEYS
    for gi in range(u_ref.shape[0] // group):
        act = jnp.dot(u_ref[gi * group:(gi + 1) * group, :], ht, preferred_element_type=F32)
        ws = []
        for r in range(2 * gi, 2 * gi + 2):
            gate = None
            for hd in range(PEER_HEADS):
                s1 = s1_ref[hd, r:r + 1, :]
                e1 = jnp.exp(s1 + st_ref[hd, 1:2, :]) * st_ref[hd, 2:3, :]
                gh = jnp.where(s2_ref[hd] + s1 >= st_ref[hd, 0:1, :], e2_ref[hd] * e1, 0.0)
                gate = gh if gate is None else gate + gh
            a = act[(r - 2 * gi) * PEER_KEYS:(r - 2 * gi + 1) * PEER_KEYS]
            gelu = 0.5 * a * (1.0 + lax.erf(a * (2.0 ** -0.5)))
            ws.append((gate * gelu).astype(BF16))
        d = jnp.dot(vt_ref[:, gi * group:(gi + 1) * group], jnp.concatenate(ws, axis=0),
                    preferred_element_type=F32)
        upd = d if upd is None else upd + d
    acc_ref[...] += upd

    @pl.when(e == pl.num_programs(2) - 1)
    def _():
        o_ref[0] = x_ref[0] + gt_ref[0] * acc_ref[...].T


def peer_expert(ht, s1, s2, e2, st, u, vt, x, gt):
    B, T, D = x.shape
    E = u.shape[0]
    tm = _row_tile(T, 256)
    nt = T // tm
    te = SUBLANES * PEER_KEYS
    tmod = tm if gt.shape[1] == T else 1
    mod_map = (lambda b, i, e: (b, i, 0)) if gt.shape[1] == T else (lambda b, i, e: (b, 0, 0))
    hk_spec = pl.BlockSpec((PEER_HEADS, PEER_KEYS, tm), lambda b, i, e: (0, 0, b * nt + i))
    return pl.pallas_call(
        _peer_expert_kernel,
        grid=(B, nt, E // te),
        in_specs=[pl.BlockSpec((D, tm), lambda b, i, e: (0, b * nt + i)),
                  pl.BlockSpec((PEER_HEADS, SUBLANES, tm), lambda b, i, e: (0, e, b * nt + i)),
                  hk_spec, hk_spec,
                  pl.BlockSpec((PEER_HEADS, SUBLANES, tm), lambda b, i, e: (0, 0, b * nt + i)),
                  pl.BlockSpec((te, D), lambda b, i, e: (e, 0)),
                  pl.BlockSpec((D, te), lambda b, i, e: (0, e)),
                  pl.BlockSpec((1, tm, D), lambda b, i, e: (b, i, 0)),
                  pl.BlockSpec((1, tmod, D), mod_map)],
        out_specs=pl.BlockSpec((1, tm, D), lambda b, i, e: (b, i, 0)),
        out_shape=jax.ShapeDtypeStruct((B, T, D), F32),
        scratch_shapes=[pltpu.VMEM((D, tm), F32)],
        compiler_params=_params("parallel", "parallel", "arbitrary"),
        name="peer_expert",
    )(ht, s1, s2, e2, st, u, vt, x, gt)


def _pack_state(s):
    B, H, dv, dk = s.shape
    return s.reshape(B, H // 2, 2, dv, dk).transpose(0, 1, 3, 2, 4).reshape(B * H // 2, dv, 2 * dk)


def _unpack_state(s, B):
    P, dv, dk2 = s.shape
    H = 2 * P // B
    return s.reshape(B, H // 2, dv, 2, dk2 // 2).transpose(0, 1, 3, 2, 4).reshape(B, H, dv, dk2 // 2)


def _pack_state_bd(s):
    B, H, dv, dk = s.shape
    st = s.transpose(0, 1, 3, 2).reshape(B, H // 2, 2, dk, dv)
    z = jnp.zeros((B, H // 2, dk, dv), s.dtype)
    top = jnp.concatenate([st[:, :, 0], z], axis=-1)
    bot = jnp.concatenate([z, st[:, :, 1]], axis=-1)
    return jnp.concatenate([top, bot], axis=-2).reshape(B * H // 2, 2 * dk, 2 * dv)


def _unpack_state_bd(s, B):
    P, dk2, dv2 = s.shape
    s = s.reshape(B, P // B, 2, dk2 // 2, 2, dv2 // 2)
    d = jnp.stack([s[:, :, 0, :, 0, :], s[:, :, 1, :, 1, :]], axis=2)
    return d.reshape(B, 2 * P // B, dk2 // 2, dv2 // 2).transpose(0, 1, 3, 2)


def _block(x, ada, pos0, past, shift_prev, state, lw, rowwise_mods):
    B, T, D = x.shape
    N = B * T
    sh1, sc1, gt1, sh2, sc2, gt2 = ada
    if rowwise_mods:
        xr = x.reshape(1, N, D)
        mod = lambda m: jnp.repeat(m, T, axis=0).reshape(1, N, D)
    else:
        xr = x
        mod = lambda m: m.reshape(B, 1, D)
    h = modulate(xr, mod(sc1), mod(sh1), lw["norm_mix"], BF16).reshape(N, D)
    q3 = matmul(h, lw["w_q3"])
    mrow = matmul(h, lw["w_moba"])
    zr = matmul(h, lw["w_rw"], tn_want=lw["w_rw"].shape[1])
    drow = matmul(h, lw["w_dsa"])
    gl = matmul(h, lw["w_gl"], out_dtype=BF16)
    MR = mrow.shape[1]
    RW = zr.shape[1]
    q3b = q3.reshape(B, T, -1)
    drow_b = drow.reshape(B, T, -1)

    if past is None:
        ident = jnp.arange(N // PAGE_SIZE, dtype=jnp.int32).reshape(B, T // PAGE_SIZE)
        kv_a, km = paged_cast(mrow.reshape(1, N // PAGE_SIZE, PAGE_SIZE, MR), 0, ident, None, MR // 2)
        kv_c = paged_cast(drow.reshape(1, N // PAGE_SIZE, PAGE_SIZE, drow.shape[1]), 0, ident, None)
        nbp = -(-km.shape[1] // 16) * 16
        km = jnp.pad(km, ((0, 0), (0, nbp - km.shape[1]), (0, 0)))
        ya = moba_attention(q3b, kv_a, km, pos0)
        yc = dsa_attention(q3b, drow_b, kv_c, pos0, pos0 + T)
    else:
        pages_a, pages_c, layer, table = past
        A = MR // 2
        ya = moba_decode(q3b[:, :, :A], pages_a, layer, table, mrow.reshape(B, T, MR))
        yc = dsa_decode(q3b[:, :, A:2 * A], q3b[:, :, 2 * A:], drow_b[:, :, DSA_ROW:DSA_ROW + IDX_HEADS],
                        pages_c, layer, table, drow_b[:, :, :DSA_ROW], pos0)

    zr_b = zr.reshape(B, T, RW)
    zs = jnp.concatenate([shift_prev[:, None], zr_b[:, :-1]], axis=1)
    prep_w = (lw["rwkv_mu"], lw["rwkv_lora"], lw["rwkv_vecs"], lw["ones_blk"])
    if T % (2 * RWKV_CHUNK) == 0:
        rs, gb, t3 = rwkv_prep(zr_b, zs, *prep_w, True)
        yb, s_new = rwkv_chunk_scan(rs, t3, gb, _pack_state_bd(state), lw["rwkv_ln_w"], lw["rwkv_ln_b"])
        s_new = _unpack_state_bd(s_new, B)
    else:
        rs, gb = rwkv_prep(zr_b.reshape(1, N, RW), zs.reshape(1, N, RW), *prep_w, False)
        Tp = -(-T // LANES) * LANES
        padt = lambda a: jnp.pad(a.reshape(B, T, -1), ((0, 0), (0, Tp - T), (0, 0)))
        A = gb.shape[2] // 2
        rs_p = padt(rs).at[:, T:, A:2 * A].set(1.0)
        yb, s_new = rwkv_scan(rs_p, padt(gb), _pack_state(state), lw["rwkv_ln_w"], lw["rwkv_ln_b"], T)
        yb = yb[:, :T]
        s_new = _unpack_state(s_new, B)

    W = ya.shape[2]
    rsh = (lambda a: a.reshape(1, N, -1)) if rowwise_mods else (lambda a: a.reshape(B, T, -1))
    x1 = merge(rsh(ya), rsh(yb), rsh(yc), rsh(gl), xr, mod(gt1), lw["w_branch"], lw["w_out"])
    ht, s1, s2, e2, st = peer_route(x1, mod(sc2), mod(sh2), lw["norm_ffn"], lw["peer_wq"], lw["peer_keys"])
    x2 = peer_expert(ht, s1, s2, e2, st, lw["peer_u"], lw["peer_vt"], x1, mod(gt2))
    return (x2.reshape(B, T, D), mrow.reshape(B, T, MR), drow_b[:, :, :DSA_ROW],
            s_new, zr_b[:, -1])


def kernel(x_prompt, x_sample, cache_moba, cache_dsa, state_rwkv, state_shift, page_table, c_prompt, c_sample, w_ada, b_ada, norm_mix, w_in, rwkv_mu, rwkv_w0, rwkv_w_up, rwkv_a0, rwkv_a_up, rwkv_k_k, rwkv_k_a, rwkv_r_k, rwkv_ln_w, rwkv_ln_b, w_branch, w_out, norm_ffn, peer_wq, peer_keys, peer_u, peer_v, norm_final):
    n_b, n_t, D = x_prompt.shape
    n_db, n_dt, _ = x_sample.shape
    depth = w_in.shape[0]
    past_len = page_table.shape[1] * PAGE_SIZE
    A = D // 2
    RW = 4 * A + 2 * B_LORA
    idx_w = IDX_HEADS * IDX_DIM
    sizes = (A, 2 * A, RW, A, idx_w, IDX_HEADS, DSA_ROW, N_BRANCH * D)
    off = np.concatenate([[0], np.cumsum(sizes)])
    assert off[-1] == w_in.shape[2]

    n_c = n_b + n_db
    n_cp = -(-n_c // SUBLANES) * SUBLANES
    c_all = jnp.pad(jnp.concatenate([c_prompt, c_sample], axis=0), ((0, n_cp - n_c), (0, 0)))
    ada_all = ada_matmul(c_all, w_ada, b_ada)

    ones_blk = jnp.asarray(np.kron(np.eye(A // D_HEAD), np.ones((D_HEAD, D_HEAD))), BF16)
    xp, xs = x_prompt, x_sample
    outs = [[] for _ in range(8)]
    for l in range(depth):
        wl = w_in[l]
        seg = lambda n: wl[:, off[n]:off[n + 1]]
        dsa_pad = LANES * (-(-(DSA_ROW + IDX_HEADS) // LANES)) - DSA_ROW - IDX_HEADS
        lora = jnp.zeros((2 * B_LORA, 2 * A), F32)
        lora = lora.at[:B_LORA, :A].set(rwkv_w_up[l]).at[B_LORA:, A:].set(rwkv_a_up[l])
        flat = lambda a: a.reshape(1, -1)
        lw = dict(
            norm_mix=norm_mix[l], norm_ffn=norm_ffn[l],
            w_q3=jnp.concatenate([seg(0), seg(3), seg(4)], axis=1).astype(BF16),
            w_moba=seg(1).astype(BF16), w_rw=seg(2).astype(BF16),
            w_dsa=jnp.pad(jnp.concatenate([seg(6), seg(5)], axis=1), ((0, 0), (0, dsa_pad))).astype(BF16),
            w_gl=seg(7).astype(BF16),
            rwkv_mu=flat(rwkv_mu[l]), rwkv_lora=lora, ones_blk=ones_blk,
            rwkv_vecs=jnp.concatenate([flat(rwkv_w0[l]), flat(rwkv_a0[l]), flat(rwkv_k_k[l]), flat(rwkv_k_a[l]),
                                       flat(rwkv_r_k[l]), jnp.zeros((SUBLANES - 5, A), F32)], axis=0),
            rwkv_ln_w=rwkv_ln_w[l], rwkv_ln_b=rwkv_ln_b[l],
            w_branch=w_branch[l].astype(BF16), w_out=w_out[l].astype(BF16),
            peer_wq=peer_wq[l].astype(BF16),
            peer_keys=peer_keys[l].reshape(2 * PEER_HEADS, PEER_KEYS, -1).astype(BF16),
            peer_u=peer_u[l].astype(BF16), peer_vt=transpose_cast(peer_v[l], BF16),
        )
        ada_p = jnp.split(ada_all[l, :n_b], 6, axis=-1)
        ada_s = jnp.split(ada_all[l, n_b:n_c], 6, axis=-1)
        xp, m_p, d_p, r_p, s_p = _block(
            xp, ada_p, 0, None, jnp.zeros((n_b, RW), F32),
            jnp.zeros((n_b, A // D_HEAD, D_HEAD, D_HEAD), F32), lw, False)
        xs, m_s, d_s, r_s, s_s = _block(
            xs, ada_s, past_len, (cache_moba, cache_dsa, l, page_table), state_shift[l], state_rwkv[l], lw, True)
        for lst, val in zip(outs, (m_p, m_s, d_p, d_s, r_p, r_s, s_p, s_s)):
            lst.append(val)
    y_prompt = rms_norm_rows(xp.reshape(n_b * n_t, D), norm_final).reshape(n_b, n_t, D)
    y_sample = rms_norm_rows(xs.reshape(n_db * n_dt, D), norm_final).reshape(n_db, n_dt, D)
    return (y_prompt, y_sample) + tuple(jnp.stack(o) for o in outs)
```
